```python
import jax, jax.numpy as jnp
from jax import lax
import numpy as np

D_MODEL = 2048
BATCH = 8
SEQ = 2048
DEPTH = 2
DEC_BATCH = 128
DEC_SEQ = 4
PAST_LEN = 2048
PAGE_SIZE = 128

N_MIXERS = 2
N_ATTN_LAYERS = (DEPTH + 1) // 2
N_CONV_LAYERS = DEPTH // 2
HEAD_DIM = 128
HEADS_PER_GROUP = D_MODEL // (2 * HEAD_DIM)
WINDOWS = (128, 512, 2048)
DILATIONS = (1, 4, 16)
N_GROUPS = 3
BLOCK = 128
ATTN_WIDTH = HEADS_PER_GROUP * HEAD_DIM
QKV_WIDTH = 3 * N_GROUPS * ATTN_WIDTH
ATTN_SCALE = HEAD_DIM ** -0.5
CONV_WIDTH = 31
D_FF = 4 * D_MODEL
RMS_EPS = 1e-6
LN_EPS = 1e-5

kernel_name = 'dilated_window_conformer_hybrid_step'


def rms_norm(x, g):
    xf = x.astype(jnp.float32)
    y = xf * lax.rsqrt(jnp.mean(xf * xf, axis=-1, keepdims=True) + RMS_EPS)
    return (y * g.astype(jnp.float32)).astype(x.dtype)


def layer_norm(x, g, b):
    xf = x.astype(jnp.float32)
    mu = jnp.mean(xf, axis=-1, keepdims=True)
    var = jnp.mean(jnp.square(xf - mu), axis=-1, keepdims=True)
    y = (xf - mu) * lax.rsqrt(var + LN_EPS) * g.astype(jnp.float32) + b.astype(jnp.float32)
    return y.astype(x.dtype)


def alibi_slopes(g):
    h = HEADS_PER_GROUP
    base = 2.0 ** (-8.0 * np.arange(1, h + 1) / h)
    return jnp.asarray((base / DILATIONS[g]).astype(np.float32))


def dilated_window_prompt(q, k, v, g):
    dil = DILATIONS[g]
    n_steps = WINDOWS[g] // dil
    slopes = alibi_slopes(g)
    B, S, H, DH = q.shape
    L = S // dil
    def by_residue(a):
        return a.reshape(B, L, dil, H, DH).transpose(0, 2, 1, 3, 4)
    nb = -(-L // BLOCK)
    Lp = nb * BLOCK
    pad = ((0, 0), (0, 0), (0, Lp - L), (0, 0), (0, 0))
    qb, kb, vb = [jnp.pad(by_residue(a), pad).reshape(B, dil, nb, BLOCK, H, DH) for a in (q, k, v)]
    def with_prev(a):
        prev = jnp.pad(a, ((0, 0), (0, 0), (1, 0), (0, 0), (0, 0), (0, 0)))[:, :, :-1]
        return jnp.concatenate([prev, a], axis=3)
    kk, vv = with_prev(kb), with_prev(vb)
    s = jnp.einsum('brnqhd,brnkhd->brnhqk', qb, kk, preferred_element_type=jnp.float32) * ATTN_SCALE
    qi = jnp.arange(BLOCK)[:, None]
    kj = jnp.arange(2 * BLOCK)[None, :]
    steps = BLOCK + qi - kj
    key_idx = jnp.arange(nb)[:, None, None] * BLOCK - BLOCK + kj[None]
    valid = (steps >= 0) & (steps <= n_steps) & (key_idx >= 0)
    bias = -slopes[:, None, None] * (dil * steps).astype(jnp.float32)[None]
    s = jnp.where(valid[:, None], s + bias, -jnp.inf)
    lse = jax.nn.logsumexp(s, axis=-1)
    p = jnp.exp(s - lse[..., None])
    o = jnp.einsum('brnhqk,brnkhd->brnqhd', p.astype(v.dtype), vv, preferred_element_type=jnp.float32)
    o = o.reshape(B, dil, Lp, H, DH)[:, :, :L].transpose(0, 2, 1, 3, 4).reshape(B, S, H, DH)
    lse = lse.transpose(0, 1, 2, 4, 3).reshape(B, dil, Lp, H)[:, :, :L]
    lse = lse.transpose(0, 2, 1, 3).reshape(B, S, H)
    return o, lse


def dilated_window_sample(q, kv_new, kv_buf, g):
    dil = DILATIONS[g]
    n_steps = WINDOWS[g] // dil
    slopes = alibi_slopes(g)
    T = q.shape[1]
    Wb = kv_buf.shape[1]
    steps = jnp.arange(n_steps + 1)
    idx = Wb + jnp.arange(T)[:, None] - dil * steps[None, :]
    valid = idx >= 0
    from_buf = (idx < Wb)[None, :, :, None, None, None]
    g_buf = kv_buf[:, jnp.clip(idx, 0, Wb - 1)]
    g_new = kv_new[:, jnp.clip(idx - Wb, 0, T - 1)]
    kv = jnp.where(from_buf, g_buf, g_new)
    k, v = kv[:, :, :, 0], kv[:, :, :, 1]
    s = jnp.einsum('bthd,btshd->bhts', q, k, preferred_element_type=jnp.float32) * ATTN_SCALE
    bias = -slopes[:, None, None] * (dil * steps).astype(jnp.float32)[None, None, :]
    s = jnp.where(valid, s + bias, -jnp.inf)
    lse = jax.nn.logsumexp(s, axis=-1)
    p = jnp.exp(s - lse[..., None])
    o = jnp.einsum('bhts,btshd->bthd', p.astype(v.dtype), v, preferred_element_type=jnp.float32)
    return o, lse.transpose(0, 2, 1)


def project_qkv(h, w_qkv):
    B, S, _ = h.shape
    return (h @ w_qkv).reshape(B, S, 3, N_GROUPS, HEADS_PER_GROUP, HEAD_DIM)


def merge_groups(outs, lses, w_o, dtype):
    alpha = jax.nn.softmax(jnp.stack(lses, axis=0), axis=0)
    o = jnp.einsum('gbsh,gbshd->bshd', alpha, jnp.stack(outs, axis=0))
    B, S = o.shape[:2]
    return o.astype(dtype).reshape(B, S, ATTN_WIDTH) @ w_o


def attention_prompt(h, w_qkv, w_o):
    qkv = project_qkv(h, w_qkv)
    S = h.shape[1]
    outs, lses, new_kv = [], [], []
    for g in range(N_GROUPS):
        o, l = dilated_window_prompt(qkv[:, :, 0, g], qkv[:, :, 1, g], qkv[:, :, 2, g], g)
        outs.append(o)
        lses.append(l)
        keep = min(WINDOWS[g], S)
        new_kv.append(qkv[:, S - keep:, 1:, g])
    return merge_groups(outs, lses, w_o, h.dtype), new_kv


def attention_sample(h, bufs, w_qkv, w_o):
    qkv = project_qkv(h, w_qkv)
    outs, lses, new_kv = [], [], []
    for g in range(N_GROUPS):
        kv_new = qkv[:, :, 1:, g]
        o, l = dilated_window_sample(qkv[:, :, 0, g], kv_new, bufs[g], g)
        outs.append(o)
        lses.append(l)
        new_kv.append(kv_new)
    return merge_groups(outs, lses, w_o, h.dtype), new_kv


def conformer_conv(h, history, w_pw1, b_pw1, w_dw, b_dw, ln_g, ln_b, w_pw2, b_pw2):
    a, b = jnp.split(h @ w_pw1 + b_pw1, 2, axis=-1)
    u = a * jax.nn.sigmoid(b)
    ext = jnp.concatenate([history.astype(u.dtype), u], axis=1)
    z = lax.conv_general_dilated(ext, w_dw[:, None, :].astype(u.dtype), (1,), 'VALID',
                                 dimension_numbers=('NWC', 'WIO', 'NWC'),
                                 feature_group_count=u.shape[-1]) + b_dw
    z = jax.nn.silu(layer_norm(z, ln_g, ln_b))
    return z @ w_pw2 + b_pw2, ext[:, -(CONV_WIDTH - 1):]


def squared_relu_mlp(x, g, w_up, w_down):
    return jnp.square(jax.nn.relu(rms_norm(x, g) @ w_up)) @ w_down


def setup_inputs(seed: int = 0) -> dict:
    key = jax.random.key(seed)
    ks = jax.random.split(key, 24)
    D = D_MODEL
    def nrm(k, shape, scale):
        return jax.random.normal(k, shape, jnp.float32) * scale
    return {
        'x_prompt': nrm(ks[0], (BATCH, SEQ, D), 1.0),
        'x_sample': nrm(ks[1], (DEC_BATCH, DEC_SEQ, D), 1.0),
        'cache_kv_g0': nrm(ks[2], (N_ATTN_LAYERS, DEC_BATCH, min(WINDOWS[0], PAST_LEN), 2, HEADS_PER_GROUP, HEAD_DIM), 1.0),
        'cache_kv_g1': nrm(ks[3], (N_ATTN_LAYERS, DEC_BATCH, min(WINDOWS[1], PAST_LEN), 2, HEADS_PER_GROUP, HEAD_DIM), 1.0),
        'cache_kv_g2': nrm(ks[4], (N_ATTN_LAYERS, DEC_BATCH, min(WINDOWS[2], PAST_LEN), 2, HEADS_PER_GROUP, HEAD_DIM), 1.0),
        'state_conv': nrm(ks[5], (N_CONV_LAYERS, DEC_BATCH, CONV_WIDTH - 1, D), 0.5),
        'attn_norm': 1.0 + nrm(ks[6], (N_ATTN_LAYERS, D), 0.02),
        'w_qkv': nrm(ks[7], (N_ATTN_LAYERS, D, QKV_WIDTH), D ** -0.5),
        'w_o': nrm(ks[8], (N_ATTN_LAYERS, ATTN_WIDTH, D), ATTN_WIDTH ** -0.5),
        'conv_norm': 1.0 + nrm(ks[9], (N_CONV_LAYERS, D), 0.02),
        'w_pw1': nrm(ks[10], (N_CONV_LAYERS, D, 2 * D), D ** -0.5),
        'b_pw1': nrm(ks[11], (N_CONV_LAYERS, 2 * D), 0.02),
        'w_dw': nrm(ks[12], (N_CONV_LAYERS, CONV_WIDTH, D), CONV_WIDTH ** -0.5),
        'b_dw': nrm(ks[13], (N_CONV_LAYERS, D), 0.02),
        'conv_ln_g': 1.0 + nrm(ks[14], (N_CONV_LAYERS, D), 0.02),
        'conv_ln_b': nrm(ks[15], (N_CONV_LAYERS, D), 0.02),
        'w_pw2': nrm(ks[16], (N_CONV_LAYERS, D, D), D ** -0.5),
        'b_pw2': nrm(ks[17], (N_CONV_LAYERS, D), 0.02),
        'mlp_norm': 1.0 + nrm(ks[18], (DEPTH, D), 0.02),
        'w_up': nrm(ks[19], (DEPTH, D, D_FF), D ** -0.5),
        'w_down': nrm(ks[20], (DEPTH, D_FF, D), D_FF ** -0.5),
        'final_norm': 1.0 + nrm(ks[21], (D,), 0.02),
    }


def reference(x_prompt, x_sample, cache_kv_g0, cache_kv_g1, cache_kv_g2, state_conv,
              attn_norm, w_qkv, w_o, conv_norm, w_pw1, b_pw1, w_dw, b_dw, conv_ln_g, conv_ln_b,
              w_pw2, b_pw2, mlp_norm, w_up, w_down, final_norm):
    caches = (cache_kv_g0, cache_kv_g1, cache_kv_g2)
    xp, xs = x_prompt, x_sample
    kv_p = [[] for _ in range(N_GROUPS)]
    kv_s = [[] for _ in range(N_GROUPS)]
    conv_p, conv_s = [], []
    for i in range(DEPTH):
        j = i // N_MIXERS
        if i % N_MIXERS == 0:
            dp, new_p = attention_prompt(rms_norm(xp, attn_norm[j]), w_qkv[j], w_o[j])
            ds, new_s = attention_sample(rms_norm(xs, attn_norm[j]), [c[j] for c in caches], w_qkv[j], w_o[j])
            for g in range(N_GROUPS):
                kv_p[g].append(new_p[g])
                kv_s[g].append(new_s[g])
        else:
            conv_w = (w_pw1[j], b_pw1[j], w_dw[j], b_dw[j], conv_ln_g[j], conv_ln_b[j], w_pw2[j], b_pw2[j])
            zero_hist = jnp.zeros((xp.shape[0], CONV_WIDTH - 1, xp.shape[2]), xp.dtype)
            dp, sp = conformer_conv(rms_norm(xp, conv_norm[j]), zero_hist, *conv_w)
            ds, ss = conformer_conv(rms_norm(xs, conv_norm[j]), state_conv[j], *conv_w)
            conv_p.append(sp)
            conv_s.append(ss)
        xp = xp + dp
        xs = xs + ds
        xp = xp + squared_relu_mlp(xp, mlp_norm[i], w_up[i], w_down[i])
        xs = xs + squared_relu_mlp(xs, mlp_norm[i], w_up[i], w_down[i])
    y_prompt = rms_norm(xp, final_norm)
    y_sample = rms_norm(xs, final_norm)
    kv_g0_prompt = jnp.stack(kv_p[0], axis=0)
    kv_g1_prompt = jnp.stack(kv_p[1], axis=0)
    kv_g2_prompt = jnp.stack(kv_p[2], axis=0)
    conv_prompt = jnp.stack(conv_p, axis=0)
    kv_g0_sample = jnp.stack(kv_s[0], axis=0)
    kv_g1_sample = jnp.stack(kv_s[1], axis=0)
    kv_g2_sample = jnp.stack(kv_s[2], axis=0)
    conv_sample = jnp.stack(conv_s, axis=0)
    return (y_prompt, y_sample, kv_g0_prompt, kv_g1_prompt, kv_g2_prompt, conv_prompt,
            kv_g0_sample, kv_g1_sample, kv_g2_sample, conv_sample)
```

```python
import functools

import numpy as np
import jax
import jax.numpy as jnp
from jax import lax
from jax.experimental import pallas as pl
from jax.experimental.pallas import tpu as pltpu

F32 = jnp.float32
BF16 = jnp.bfloat16

D_MODEL = 2048
BATCH = 8
SEQ = 2048
DEC_BATCH = 128
DEC_SEQ = 4
HEAD_DIM = 128
N_HEADS = 8
ATTN_WIDTH = N_HEADS * HEAD_DIM
KV_W = 2 * ATTN_WIDTH
WINDOWS = (128, 512, 2048)
DILATIONS = (1, 4, 16)
N_STEPS = 128
ATTN_SCALE = HEAD_DIM ** -0.5
CONV_WIDTH = 31
HIST = CONV_WIDTH - 1
D_FF = 4 * D_MODEL
RMS_EPS = 1e-6
LN_EPS = 1e-5
NEG = -1e30

MP = BATCH * SEQ
MS = DEC_BATCH * DEC_SEQ

V7X_VMEM_BYTES = 64 * 1024 * 1024
VMEM_HEADROOM_BYTES = 16 * 1024 * 1024


def _params(semantics, vmem_estimate):
    limit = min(int(vmem_estimate) + VMEM_HEADROOM_BYTES, V7X_VMEM_BYTES - 4 * 1024 * 1024)
    return pltpu.CompilerParams(dimension_semantics=semantics, vmem_limit_bytes=limit)


def _rms(x, g):
    y = x * lax.rsqrt(jnp.mean(x * x, axis=-1, keepdims=True) + RMS_EPS)
    return y * g


def _head_slopes():
    return (2.0 ** (-8.0 * np.arange(1, N_HEADS + 1) / N_HEADS)).astype(np.float32)


def _qkv_kernel(x_ref, g_ref, w_ref, q_ref, kv01_ref, kv2_ref, xn_ref):
    j = pl.program_id(1)

    @pl.when(j == 0)
    def _():
        xn_ref[...] = _rms(x_ref[...], g_ref[...]).astype(BF16)

    acc = jnp.dot(xn_ref[...], w_ref[...], preferred_element_type=F32)

    @pl.when(j < 3)
    def _():
        q_ref[0] = acc.astype(BF16)

    @pl.when(jnp.logical_and(j >= 3, j < 7))
    def _():
        kv01_ref[0] = acc

    @pl.when(j >= 7)
    def _():
        kv2_ref[...] = acc


def _qkv_wcol(j):
    jj = jnp.maximum(j - 3, 0)
    return jnp.where(j < 3, j, (1 + jj % 2) * 3 + jj // 2)


def _qkv(x, gain, w, tm):
    m = x.shape[0]
    tn = ATTN_WIDTH

    def kv01_idx(i, j):
        jj = jnp.clip(j - 3, 0, 3)
        return (jj // 2, i, jj % 2)

    est = 2 * (tm * D_MODEL * 4 + D_MODEL * tn * 2 + tm * tn * 2 + 2 * tm * tn * 4) + tm * D_MODEL * 2
    return pl.pallas_call(
        _qkv_kernel,
        grid=(m // tm, 9),
        in_specs=[
            pl.BlockSpec((tm, D_MODEL), lambda i, j: (i, 0)),
            pl.BlockSpec((1, D_MODEL), lambda i, j: (0, 0)),
            pl.BlockSpec((D_MODEL, tn), lambda i, j: (0, _qkv_wcol(j))),
        ],
        out_specs=[
            pl.BlockSpec((1, tm, tn), lambda i, j: (jnp.minimum(j, 2), i, 0)),
            pl.BlockSpec((1, tm, tn), kv01_idx),
            pl.BlockSpec((tm, tn), lambda i, j: (i, jnp.clip(j - 7, 0, 1))),
        ],
        out_shape=[
            jax.ShapeDtypeStruct((3, m, tn), BF16),
            jax.ShapeDtypeStruct((2, m, 2 * tn), F32),
            jax.ShapeDtypeStruct((m, 2 * tn), F32),
        ],
        scratch_shapes=[pltpu.VMEM((tm, D_MODEL), BF16)],
        compiler_params=_params(("parallel", "arbitrary"), est),
        name="rms_qkv",
    )(x, gain.reshape(1, D_MODEL), w)


ATTN_HB = 1
ATTN_HW = ATTN_HB * HEAD_DIM
QT = 256
N_QT = SEQ // QT
TAB_DIAG0, TAB_EDGE0, TAB_DIAG1, TAB_MID1, TAB_EDGE1, TAB_DIAG2, TAB_PER2 = range(7)
N_TABS = 7


def _prompt_bias_tables():
    base = jnp.asarray(_head_slopes())
    d0 = jnp.arange(QT, dtype=jnp.int32)[:, None] - jnp.arange(QT, dtype=jnp.int32)[None, :]

    def table(d, g):
        dil, window = DILATIONS[g], WINDOWS[g]
        valid = (d >= 0) & (d <= window) & (d % dil == 0)
        slope = (base / dil)[:, None, None]
        return jnp.where(valid[None], -slope * d.astype(F32)[None], NEG)

    slope2 = (base / DILATIONS[2])[:, None, None]
    periodic = jnp.where((d0 % DILATIONS[2] == 0)[None], -slope2 * d0.astype(F32)[None], NEG)
    tabs = jnp.stack([table(d0, 0), table(d0 + QT, 0), table(d0, 1), table(d0 + QT, 1), table(d0 + 2 * QT, 1),
                      table(d0, 2), periodic], axis=1)
    offs = -slope2 * (QT * jnp.arange(N_QT, dtype=F32))[None, :, None]
    return tabs, jnp.broadcast_to(offs, (N_HEADS, N_QT, 128))


def _prompt_schedule(qa):
    sched = [(0, 0, TAB_DIAG0, False)]
    if qa >= 1:
        sched.append((0, 1, TAB_EDGE0, False))
    sched.append((1, 0, TAB_DIAG1, False))
    if qa >= 1:
        sched.append((1, 1, TAB_MID1, False))
    if qa >= 2:
        sched.append((1, 2, TAB_EDGE1, False))
    sched.append((2, 0, TAB_DIAG2, False))
    for a in range(1, qa + 1):
        sched.append((2, a, TAB_PER2, True))
    return sched


def _attn_prompt_kernel(q_ref, k01_ref, v01_ref, k2_ref, v2_ref, tab_ref, off_ref, o_ref, kt_ref, vb_ref):
    for g in range(3):
        k_src = k01_ref.at[g] if g < 2 else k2_ref
        v_src = v01_ref.at[g] if g < 2 else v2_ref
        for j in range(N_QT):
            keys = slice(j * QT, (j + 1) * QT)
            vb_ref[g, keys, :] = v_src[keys, :].astype(BF16)
            for h in range(ATTN_HB):
                kt_ref[g, h, :, keys] = k_src[keys, h * HEAD_DIM:(h + 1) * HEAD_DIM].T.astype(BF16)

    for h in range(ATTN_HB):
        hc = slice(h * HEAD_DIM, (h + 1) * HEAD_DIM)
        for qa in range(N_QT):
            rows = slice(qa * QT, (qa + 1) * QT)
            m = l = acc = None
            for g, a, tab, periodic in _prompt_schedule(qa):
                keys = slice((qa - a) * QT, (qa - a + 1) * QT)
                s = jnp.dot(q_ref[g, rows, hc], kt_ref[g, h, :, keys], preferred_element_type=F32)
                s = s * ATTN_SCALE + tab_ref[h, tab]
                rmax = jnp.max(s, axis=-1, keepdims=True)
                if periodic:
                    off = off_ref[h, a:a + 1, 0:1]
                    rmax = rmax + off
                m_new = rmax if m is None else jnp.maximum(m, rmax)
                p = jnp.exp(s - ((m_new - off) if periodic else m_new))
                psum = jnp.sum(p, axis=-1, keepdims=True)
                pv = jnp.dot(p.astype(BF16), vb_ref[g, keys, hc], preferred_element_type=F32)
                if m is None:
                    l, acc = psum, pv
                else:
                    alpha = jnp.exp(m - m_new)
                    l = alpha * l + psum
                    acc = alpha * acc + pv
                m = m_new
            o_ref[rows, hc] = (acc * (1.0 / l)).astype(BF16)


def _attn_prompt(q3, kv01, kv2):
    tabs, offs = _prompt_bias_tables()
    nhh = N_HEADS // ATTN_HB
    vcol = ATTN_WIDTH // ATTN_HW
    est = (2 * (3 * SEQ * ATTN_HW * 2 + 6 * SEQ * ATTN_HW * 4 + ATTN_HB * N_TABS * QT * QT * 4 + SEQ * ATTN_HW * 2)
           + 6 * SEQ * ATTN_HW * 2)
    o = pl.pallas_call(
        _attn_prompt_kernel,
        grid=(BATCH, nhh),
        in_specs=[
            pl.BlockSpec((3, None, SEQ, ATTN_HW), lambda b, hh: (0, b, 0, hh)),
            pl.BlockSpec((2, None, SEQ, ATTN_HW), lambda b, hh: (0, b, 0, hh)),
            pl.BlockSpec((2, None, SEQ, ATTN_HW), lambda b, hh: (0, b, 0, vcol + hh)),
            pl.BlockSpec((None, SEQ, ATTN_HW), lambda b, hh: (b, 0, hh)),
            pl.BlockSpec((None, SEQ, ATTN_HW), lambda b, hh: (b, 0, vcol + hh)),
            pl.BlockSpec((ATTN_HB, N_TABS, QT, QT), lambda b, hh: (hh, 0, 0, 0)),
            pl.BlockSpec((ATTN_HB, N_QT, 128), lambda b, hh: (hh, 0, 0)),
        ],
        out_specs=pl.BlockSpec((None, SEQ, ATTN_HW), lambda b, hh: (b, 0, hh)),
        out_shape=jax.ShapeDtypeStruct((BATCH, SEQ, ATTN_WIDTH), BF16),
        scratch_shapes=[pltpu.VMEM((3, ATTN_HB, HEAD_DIM, SEQ), BF16), pltpu.VMEM((3, SEQ, ATTN_HW), BF16)],
        compiler_params=_params(("parallel", "arbitrary"), est),
        name="attn_prompt",
    )(q3.reshape(3, BATCH, SEQ, ATTN_WIDTH), kv01.reshape(2, BATCH, SEQ, KV_W), kv01.reshape(2, BATCH, SEQ, KV_W),
      kv2.reshape(BATCH, SEQ, KV_W), kv2.reshape(BATCH, SEQ, KV_W), tabs, offs)
    return o.reshape(MP, ATTN_WIDTH)


def _wo_kernel(o_ref, x_ref, w_ref, out_ref):
    out_ref[...] = x_ref[...] + jnp.dot(o_ref[...].astype(BF16), w_ref[...], preferred_element_type=F32)


def _wo(o, x, w, tm):
    m = x.shape[0]
    est = 2 * (tm * ATTN_WIDTH * 4 + 2 * tm * D_MODEL * 4 + ATTN_WIDTH * D_MODEL * 2)
    row = lambda i: (i, 0)
    return pl.pallas_call(
        _wo_kernel,
        grid=(m // tm,),
        in_specs=[pl.BlockSpec((tm, ATTN_WIDTH), row), pl.BlockSpec((tm, D_MODEL), row),
                  pl.BlockSpec((ATTN_WIDTH, D_MODEL), lambda i: (0, 0))],
        out_specs=pl.BlockSpec((tm, D_MODEL), row),
        out_shape=jax.ShapeDtypeStruct((m, D_MODEL), F32),
        compiler_params=_params(("parallel",), est),
        name="wo_residual",
    )(o, x, w)


S_COLS = DEC_SEQ * N_HEADS
SAMPLE_BB = 2
G0_TOKENS = WINDOWS[0]
GD_TOKENS = DEC_SEQ * N_STEPS


def _attn_sample_kernel(q_ref, kvn_ref, c0_ref, c1_ref, c2_ref, t0_ref, td_ref, tn_ref, e_ref, o_ref):
    nt = (((1,), (1,)), ((), ()))
    e_all = e_ref[DEC_SEQ]
    sub = lax.broadcasted_iota(jnp.int32, (N_HEADS, S_COLS), 0)
    col = lax.broadcasted_iota(jnp.int32, (N_HEADS, S_COLS), 1)
    tabs = (t0_ref, td_ref, td_ref)
    n_tok = (G0_TOKENS, GD_TOKENS, GD_TOKENS)

    def cached(g, bi, part):
        if g == 0:
            x = c0_ref[bi, :, part, :, :]
        elif g == 1:
            x = c1_ref[bi, :, part, :, :]
        else:
            x = c2_ref[bi, :, :, part, :, :].reshape(GD_TOKENS, N_HEADS, HEAD_DIM)
        return x.reshape(n_tok[g] * N_HEADS, HEAD_DIM)

    for bi in range(SAMPLE_BB):
        tok = slice(bi * DEC_SEQ, (bi + 1) * DEC_SEQ)
        s_buf, s_new = [], []
        m_all = None
        for g in range(3):
            q = q_ref[g, bi * S_COLS:(bi + 1) * S_COLS, :]
            s = lax.dot_general(cached(g, bi, 0).astype(BF16), q, nt, preferred_element_type=F32)
            s = (s * ATTN_SCALE + tabs[g][...]).reshape(n_tok[g], N_HEADS, S_COLS)
            kn = kvn_ref[g, tok, 0, :, :].reshape(S_COLS, HEAD_DIM)
            sn = lax.dot_general(kn.astype(BF16), q, nt, preferred_element_type=F32)
            sn = (sn * ATTN_SCALE + tn_ref[min(g, 1)]).reshape(DEC_SEQ, N_HEADS, S_COLS)
            s_buf.append(s)
            s_new.append(sn)
            cm = jnp.maximum(jnp.max(s, axis=0), jnp.max(sn, axis=0))
            m_all = cm if m_all is None else jnp.maximum(m_all, cm)

        m_all = jnp.where(col % N_HEADS == sub, m_all, 0.0)
        l_all = jnp.zeros((N_HEADS, S_COLS), F32)
        acc = [jnp.zeros((N_HEADS, HEAD_DIM), F32) for _ in range(DEC_SEQ)]
        for g in range(3):
            p = jnp.exp(s_buf[g] - m_all[None])
            pn = jnp.exp(s_new[g] - m_all[None])
            l_all = l_all + jnp.sum(p, axis=0) + jnp.sum(pn, axis=0)
            pb = p.reshape(n_tok[g] * N_HEADS, S_COLS).astype(BF16)
            pnb = pn.reshape(S_COLS, S_COLS).astype(BF16)
            v = cached(g, bi, 1)
            vn = kvn_ref[g, tok, 1, :, :].reshape(S_COLS, HEAD_DIM)
            if g == 0:
                for t in range(DEC_SEQ):
                    w = jnp.dot(pb, e_ref[t], preferred_element_type=F32) * v
                    wn = jnp.dot(pnb, e_ref[t], preferred_element_type=F32) * vn
                    acc[t] = (acc[t] + jnp.sum(w.reshape(G0_TOKENS, N_HEADS, HEAD_DIM), axis=0)
                              + jnp.sum(wn.reshape(DEC_SEQ, N_HEADS, HEAD_DIM), axis=0))
            else:
                w = jnp.dot(pb, e_all, preferred_element_type=F32) * v
                wn = jnp.dot(pnb, e_all, preferred_element_type=F32) * vn
                wq = jnp.sum(w.reshape(N_STEPS, DEC_SEQ, N_HEADS, HEAD_DIM), axis=0)
                wnq = wn.reshape(DEC_SEQ, N_HEADS, HEAD_DIM)
                for t in range(DEC_SEQ):
                    acc[t] = acc[t] + wq[t] + wnq[t]

        inv = 1.0 / l_all
        for t in range(DEC_SEQ):
            inv_t = jnp.sum(jnp.where(col == t * N_HEADS + sub, inv, 0.0), axis=-1, keepdims=True)
            o_ref[bi * DEC_SEQ + t] = acc[t] * inv_t


def _sample_tables():
    base = _head_slopes()
    cols = np.arange(S_COLS)
    ct, ch = (cols // N_HEADS)[None, :], (cols % N_HEADS)[None, :]
    slope = base[cols % N_HEADS][None, :]

    def rows(n_tokens):
        r = np.arange(n_tokens * N_HEADS)[:, None]
        return r // N_HEADS, r % N_HEADS

    m, h = rows(G0_TOKENS)
    t0 = np.where((h == ch) & (m >= ct), -slope * (G0_TOKENS + ct - m), NEG)
    j, h = rows(DEC_SEQ)
    tn0 = np.where((h == ch) & (j <= ct), -slope * (ct - j), NEG)
    tnd = np.where((h == ch) & (j == ct), 0.0, NEG)
    n, h = rows(GD_TOKENS)
    td = np.where((h == ch) & (n % DEC_SEQ == ct), -slope * (N_STEPS - n // DEC_SEQ), NEG)
    expand = np.zeros((DEC_SEQ + 1, S_COLS, HEAD_DIM), np.float32)
    for t in range(DEC_SEQ):
        expand[t, t * N_HEADS:(t + 1) * N_HEADS, :] = 1.0
    expand[DEC_SEQ] = 1.0
    f = lambda a: jnp.asarray(a.astype(np.float32))
    return f(t0), f(td), f(np.stack([tn0, tnd])), jnp.asarray(expand, dtype=BF16)


def _attn_sample(q3, kv01, kv2, cache0, cache1, cache2):
    t0, td, tn, expand = _sample_tables()
    kvn = jnp.concatenate([kv01, kv2[None]], axis=0).reshape(3, MS, 2, N_HEADS, HEAD_DIM)
    q = q3.reshape(3, MS * N_HEADS, HEAD_DIM)
    c2v = cache2.reshape(1, DEC_BATCH, N_STEPS, DILATIONS[2], 2, N_HEADS, HEAD_DIM)
    bb = SAMPLE_BB
    est = 2 * bb * (G0_TOKENS + 2 * GD_TOKENS) * KV_W * 4 + 16 * 1024 * 1024
    full = lambda a: pl.BlockSpec(a.shape, lambda i: (0,) * a.ndim)
    o = pl.pallas_call(
        _attn_sample_kernel,
        grid=(DEC_BATCH // bb,),
        in_specs=[
            pl.BlockSpec((3, bb * S_COLS, HEAD_DIM), lambda i: (0, i, 0)),
            pl.BlockSpec((3, bb * DEC_SEQ, 2, N_HEADS, HEAD_DIM), lambda i: (0, i, 0, 0, 0)),
            pl.BlockSpec((None, bb, G0_TOKENS, 2, N_HEADS, HEAD_DIM), lambda i: (0, i, 0, 0, 0, 0)),
            pl.BlockSpec((None, bb, GD_TOKENS, 2, N_HEADS, HEAD_DIM), lambda i: (0, i, 0, 0, 0, 0)),
            pl.BlockSpec((None, bb, N_STEPS, DEC_SEQ, 2, N_HEADS, HEAD_DIM), lambda i: (0, i, 0, 0, 0, 0, 0)),
            full(t0), full(td), full(tn), full(expand),
        ],
        out_specs=pl.BlockSpec((bb * DEC_SEQ, N_HEADS, HEAD_DIM), lambda i: (i, 0, 0)),
        out_shape=jax.ShapeDtypeStruct((MS, N_HEADS, HEAD_DIM), F32),
        compiler_params=_params(("parallel",), est),
        name="attn_sample",
    )(q, kvn, cache0, cache1, c2v, t0, td, tn, expand)
    return o.reshape(MS, ATTN_WIDTH)


def _mlp_kernel(x_ref, g_ref, wu_ref, wd_ref, gf_ref, out_ref, xn_ref, acc_ref, *, final):
    f = pl.program_id(1)

    @pl.when(f == 0)
    def _():
        xn_ref[...] = _rms(x_ref[...], g_ref[...]).astype(BF16)
        acc_ref[...] = jnp.zeros_like(acc_ref)

    h = jnp.dot(xn_ref[...], wu_ref[...], preferred_element_type=F32)
    h = jnp.maximum(h, 0.0)
    h = (h * h).astype(BF16)
    acc_ref[...] += jnp.dot(h, wd_ref[...], preferred_element_type=F32)

    @pl.when(f == pl.num_programs(1) - 1)
    def _():
        y = x_ref[...] + acc_ref[...]
        if final:
            y = _rms(y, gf_ref[...])
        out_ref[...] = y


def _mlp(x, gain, w_up, w_down, final_gain, tm, tf, final):
    m = x.shape[0]
    est = 2 * (2 * tm * D_MODEL * 4 + 2 * D_MODEL * tf * 2) + tm * D_MODEL * 6 + tm * tf * 8
    return pl.pallas_call(
        functools.partial(_mlp_kernel, final=final),
        grid=(m // tm, D_FF // tf),
        in_specs=[
            pl.BlockSpec((tm, D_MODEL), lambda i, f: (i, 0)),
            pl.BlockSpec((1, D_MODEL), lambda i, f: (0, 0)),
            pl.BlockSpec((D_MODEL, tf), lambda i, f: (0, f)),
            pl.BlockSpec((tf, D_MODEL), lambda i, f: (f, 0)),
            pl.BlockSpec((1, D_MODEL), lambda i, f: (0, 0)),
        ],
        out_specs=pl.BlockSpec((tm, D_MODEL), lambda i, f: (i, 0)),
        out_shape=jax.ShapeDtypeStruct((m, D_MODEL), F32),
        scratch_shapes=[pltpu.VMEM((tm, D_MODEL), BF16), pltpu.VMEM((tm, D_MODEL), F32)],
        compiler_params=_params(("parallel", "arbitrary"), est),
        name="mlp_final" if final else "mlp",
    )(x, gain.reshape(1, D_MODEL), w_up, w_down, final_gain.reshape(1, D_MODEL))


def _pw1_kernel(x_ref, g_ref, wa_ref, wb_ref, ba_ref, bb_ref, u_ref, xn_ref):
    @pl.when(pl.program_id(1) == 0)
    def _():
        xn_ref[...] = _rms(x_ref[...], g_ref[...]).astype(BF16)

    xn = xn_ref[...]
    a = jnp.dot(xn, wa_ref[...], preferred_element_type=F32) + ba_ref[...]
    b = jnp.dot(xn, wb_ref[...], preferred_element_type=F32) + bb_ref[...]
    u_ref[...] = a * (1.0 / (1.0 + jnp.exp(-b)))


def _pw1_glu(x, gain, w, bias, tm, tn):
    m = x.shape[0]
    nj = D_MODEL // tn
    b2 = bias.reshape(1, 2 * D_MODEL)
    est = 2 * (tm * D_MODEL * 4 + 2 * D_MODEL * tn * 2 + tm * tn * 4) + tm * D_MODEL * 2 + 2 * tm * tn * 4
    return pl.pallas_call(
        _pw1_kernel,
        grid=(m // tm, nj),
        in_specs=[
            pl.BlockSpec((tm, D_MODEL), lambda i, j: (i, 0)),
            pl.BlockSpec((1, D_MODEL), lambda i, j: (0, 0)),
            pl.BlockSpec((D_MODEL, tn), lambda i, j: (0, j)),
            pl.BlockSpec((D_MODEL, tn), lambda i, j: (0, nj + j)),
            pl.BlockSpec((1, tn), lambda i, j: (0, j)),
            pl.BlockSpec((1, tn), lambda i, j: (0, nj + j)),
        ],
        out_specs=pl.BlockSpec((tm, tn), lambda i, j: (i, j)),
        out_shape=jax.ShapeDtypeStruct((m, D_MODEL), F32),
        scratch_shapes=[pltpu.VMEM((tm, D_MODEL), BF16)],
        compiler_params=_params(("parallel", "arbitrary"), est),
        name="pw1_glu",
    )(x, gain.reshape(1, D_MODEL), w, w, b2, b2)


CONV_TS = 256
CONV_HALO = 32


def _dw_prompt_kernel(prev_ref, u_ref, w_ref, b_ref, z_ref, ext_ref):
    i = pl.program_id(1)

    @pl.when(i == 0)
    def _():
        ext_ref[0:CONV_HALO, :] = jnp.zeros((CONV_HALO, D_MODEL), F32)

    @pl.when(i > 0)
    def _():
        ext_ref[0:CONV_HALO, :] = prev_ref[...]

    ext_ref[CONV_HALO:, :] = u_ref[...]
    off = CONV_HALO - HIST

    def chunk(c, carry):
        cs = pl.ds(pl.multiple_of(c * 128, 128), 128)
        acc = jnp.zeros((CONV_TS, 128), F32) + b_ref[:, cs]
        for phase in range(8):
            span = CONV_TS + CONV_HALO - (0 if phase == 0 else 8)
            shifted = ext_ref[pl.ds(phase, span), cs]
            for a in range((CONV_HALO + 8) // 8):
                j = 8 * a + phase - off
                if 0 <= j < CONV_WIDTH:
                    acc = acc + shifted[8 * a:8 * a + CONV_TS, :] * w_ref[pl.ds(j, 1), cs]
        z_ref[:, cs] = acc
        return carry

    lax.fori_loop(0, D_MODEL // 128, chunk, 0)


def _dw_prompt(u, w_dw, b_dw):
    u3 = u.reshape(BATCH, SEQ, D_MODEL)
    nt = SEQ // CONV_TS
    per = CONV_TS // CONV_HALO
    est = 2 * (CONV_HALO + 2 * CONV_TS) * D_MODEL * 4 + (CONV_TS + CONV_HALO) * D_MODEL * 4 + 2 * 32 * D_MODEL * 4
    z = pl.pallas_call(
        _dw_prompt_kernel,
        grid=(BATCH, nt),
        in_specs=[
            pl.BlockSpec((None, CONV_HALO, D_MODEL), lambda b, i: (b, jnp.maximum(i * per - 1, 0), 0)),
            pl.BlockSpec((None, CONV_TS, D_MODEL), lambda b, i: (b, i, 0)),
            pl.BlockSpec((CONV_WIDTH, D_MODEL), lambda b, i: (0, 0)),
            pl.BlockSpec((1, D_MODEL), lambda b, i: (0, 0)),
        ],
        out_specs=pl.BlockSpec((None, CONV_TS, D_MODEL), lambda b, i: (b, i, 0)),
        out_shape=jax.ShapeDtypeStruct((BATCH, SEQ, D_MODEL), F32),
        scratch_shapes=[pltpu.VMEM((CONV_TS + CONV_HALO, D_MODEL), F32)],
        compiler_params=_params(("parallel", "arbitrary"), est),
        name="dwconv_prompt",
    )(u3, u3, w_dw, b_dw.reshape(1, D_MODEL))
    return z.reshape(MP, D_MODEL)


DW_BB = 8


def _dw_sample_kernel(hist_ref, u_ref, wh_ref, wu_ref, b_ref, z_ref):
    for b in range(DW_BB):
        hist = hist_ref[b]
        u = u_ref[b]
        for t in range(DEC_SEQ):
            z = jnp.sum(wh_ref[t] * hist, axis=0, keepdims=True)
            z = z + jnp.sum(wu_ref[t] * u, axis=0, keepdims=True)
            z_ref[b, t:t + 1, :] = z + b_ref[...]


def _dw_sample(hist, u, w_dw, b_dw):
    zeros = jnp.zeros((DEC_SEQ, D_MODEL), F32)
    wh = jnp.stack([jnp.concatenate([zeros[:t], w_dw[:HIST - t]], axis=0) for t in range(DEC_SEQ)])
    wu = jnp.stack([jnp.concatenate([w_dw[HIST - t:], zeros[:DEC_SEQ - 1 - t]], axis=0) for t in range(DEC_SEQ)])
    u3 = u.reshape(DEC_BATCH, DEC_SEQ, D_MODEL)
    est = 4 * (DW_BB * (32 + 8 + 8) * D_MODEL * 4) + 2 * DEC_SEQ * 40 * D_MODEL * 4
    z = pl.pallas_call(
        _dw_sample_kernel,
        grid=(DEC_BATCH // DW_BB,),
        in_specs=[
            pl.BlockSpec((DW_BB, HIST, D_MODEL), lambda i: (i, 0, 0)),
            pl.BlockSpec((DW_BB, DEC_SEQ, D_MODEL), lambda i: (i, 0, 0)),
            pl.BlockSpec((DEC_SEQ, HIST, D_MODEL), lambda i: (0, 0, 0)),
            pl.BlockSpec((DEC_SEQ, DEC_SEQ, D_MODEL), lambda i: (0, 0, 0)),
            pl.BlockSpec((1, D_MODEL), lambda i: (0, 0)),
        ],
        out_specs=pl.BlockSpec((DW_BB, DEC_SEQ, D_MODEL), lambda i: (i, 0, 0)),
        out_shape=jax.ShapeDtypeStruct((DEC_BATCH, DEC_SEQ, D_MODEL), F32),
        compiler_params=_params(("parallel",), est),
        name="dwconv_sample",
    )(hist, u3, wh, wu, b_dw.reshape(1, D_MODEL))
    return z.reshape(MS, D_MODEL)


def _ln_pw2_kernel(z_ref, x_ref, lg_ref, lb_ref, w_ref, b_ref, out_ref):
    z = z_ref[...]
    mu = jnp.mean(z, axis=-1, keepdims=True)
    zc = z - mu
    var = jnp.mean(zc * zc, axis=-1, keepdims=True)
    y = zc * lax.rsqrt(var + LN_EPS) * lg_ref[...] + lb_ref[...]
    y = y * (1.0 / (1.0 + jnp.exp(-y)))
    out_ref[...] = x_ref[...] + jnp.dot(y.astype(BF16), w_ref[...], preferred_element_type=F32) + b_ref[...]


def _ln_pw2(z, x, ln_g, ln_b, w, bias, tm):
    m = x.shape[0]
    est = 2 * (3 * tm * D_MODEL * 4 + D_MODEL * D_MODEL * 2) + 2 * tm * D_MODEL * 4
    row = lambda i: (i, 0)
    vec = pl.BlockSpec((1, D_MODEL), lambda i: (0, 0))
    return pl.pallas_call(
        _ln_pw2_kernel,
        grid=(m // tm,),
        in_specs=[pl.BlockSpec((tm, D_MODEL), row), pl.BlockSpec((tm, D_MODEL), row), vec, vec,
                  pl.BlockSpec((D_MODEL, D_MODEL), lambda i: (0, 0)), vec],
        out_specs=pl.BlockSpec((tm, D_MODEL), row),
        out_shape=jax.ShapeDtypeStruct((m, D_MODEL), F32),
        compiler_params=_params(("parallel",), est),
        name="ln_swish_pw2",
    )(z, x, ln_g.reshape(1, D_MODEL), ln_b.reshape(1, D_MODEL), w, bias.reshape(1, D_MODEL))


def kernel(x_prompt, x_sample, cache_kv_g0, cache_kv_g1, cache_kv_g2, state_conv, attn_norm, w_qkv, w_o,
           conv_norm, w_pw1, b_pw1, w_dw, b_dw, conv_ln_g, conv_ln_b, w_pw2, b_pw2, mlp_norm, w_up, w_down,
           final_norm):
    xp = x_prompt.reshape(MP, D_MODEL)
    xs = x_sample.reshape(MS, D_MODEL)

    wqkv = w_qkv[0].astype(BF16)
    wo = w_o[0].astype(BF16)
    q3p, kv01p, kv2p = _qkv(xp, attn_norm[0], wqkv, tm=1024)
    q3s, kv01s, kv2s = _qkv(xs, attn_norm[0], wqkv, tm=MS)

    o_p = _attn_prompt(q3p, kv01p, kv2p)
    xp = _wo(o_p, xp, wo, tm=512)

    o_s = _attn_sample(q3s, kv01s, kv2s, cache_kv_g0, cache_kv_g1, cache_kv_g2)
    xs = _wo(o_s, xs, wo, tm=MS)

    wu0, wd0 = w_up[0].astype(BF16), w_down[0].astype(BF16)
    xp = _mlp(xp, mlp_norm[0], wu0, wd0, final_norm, tm=512, tf=512, final=False)
    xs = _mlp(xs, mlp_norm[0], wu0, wd0, final_norm, tm=MS, tf=512, final=False)

    w1 = w_pw1[0].astype(BF16)
    w2 = w_pw2[0].astype(BF16)
    up = _pw1_glu(xp, conv_norm[0], w1, b_pw1[0], tm=1024, tn=512)
    us = _pw1_glu(xs, conv_norm[0], w1, b_pw1[0], tm=MS, tn=512)
    zp = _dw_prompt(up, w_dw[0], b_dw[0])
    zs = _dw_sample(state_conv[0], us, w_dw[0], b_dw[0])
    xp = _ln_pw2(zp, xp, conv_ln_g[0], conv_ln_b[0], w2, b_pw2[0], tm=512)
    xs = _ln_pw2(zs, xs, conv_ln_g[0], conv_ln_b[0], w2, b_pw2[0], tm=MS)

    wu1, wd1 = w_up[1].astype(BF16), w_down[1].astype(BF16)
    yp = _mlp(xp, mlp_norm[1], wu1, wd1, final_norm, tm=512, tf=512, final=True)
    ys = _mlp(xs, mlp_norm[1], wu1, wd1, final_norm, tm=MS, tf=512, final=True)

    def tail(kv, keep):
        return kv.reshape(BATCH, SEQ, KV_W)[:, SEQ - keep:].reshape(1, BATCH, keep, 2, N_HEADS, HEAD_DIM)

    def new_rows(kv):
        return kv.reshape(1, DEC_BATCH, DEC_SEQ, 2, N_HEADS, HEAD_DIM)

    conv_prompt = up.reshape(BATCH, SEQ, D_MODEL)[:, SEQ - HIST:][None]
    conv_sample = jnp.concatenate([state_conv[0][:, DEC_SEQ:], us.reshape(DEC_BATCH, DEC_SEQ, D_MODEL)], axis=1)[None]
    return (yp.reshape(BATCH, SEQ, D_MODEL), ys.reshape(DEC_BATCH, DEC_SEQ, D_MODEL),
            tail(kv01p[0], WINDOWS[0]), tail(kv01p[1], WINDOWS[1]), tail(kv2p, WINDOWS[2]),
            conv_prompt,
            new_rows(kv01s[0]), new_rows(kv01s[1]), new_rows(kv2s),
            conv_sample)
```

```python
import functools

import numpy as np
import jax
import jax.numpy as jnp
from jax import lax
from jax.experimental import pallas as pl
from jax.experimental.pallas import tpu as pltpu

F32 = jnp.float32
BF16 = jnp.bfloat16

D_MODEL = 2048
BATCH = 8
SEQ = 2048
DEC_BATCH = 128
DEC_SEQ = 4
HEAD_DIM = 128
N_HEADS = 8
ATTN_WIDTH = N_HEADS * HEAD_DIM
KV_W = 2 * ATTN_WIDTH
WINDOWS = (128, 512, 2048)
DILATIONS = (1, 4, 16)
N_STEPS = 128
ATTN_SCALE = HEAD_DIM ** -0.5
CONV_WIDTH = 31
HIST = CONV_WIDTH - 1
D_FF = 4 * D_MODEL
RMS_EPS = 1e-6
LN_EPS = 1e-5
NEG = -1e30

MP = BATCH * SEQ
MS = DEC_BATCH * DEC_SEQ

V7X_VMEM_BYTES = 64 * 1024 * 1024
VMEM_HEADROOM_BYTES = 16 * 1024 * 1024


def _params(semantics, vmem_estimate):
    limit = min(int(vmem_estimate) + VMEM_HEADROOM_BYTES, V7X_VMEM_BYTES - 4 * 1024 * 1024)
    return pltpu.CompilerParams(dimension_semantics=semantics, vmem_limit_bytes=limit)


def _rms(x, g):
    y = x * lax.rsqrt(jnp.mean(x * x, axis=-1, keepdims=True) + RMS_EPS)
    return y * g


def _head_slopes():
    return (2.0 ** (-8.0 * np.arange(1, N_HEADS + 1) / N_HEADS)).astype(np.float32)


def _qkv_kernel(x_ref, g_ref, w_ref, q_ref, kv01_ref, kv2_ref, xn_ref):
    j = pl.program_id(1)

    @pl.when(j == 0)
    def _():
        xn_ref[...] = _rms(x_ref[...], g_ref[...]).astype(BF16)

    acc = jnp.dot(xn_ref[...], w_ref[...], preferred_element_type=F32)

    @pl.when(j < 3)
    def _():
        q_ref[0] = acc

    @pl.when(jnp.logical_and(j >= 3, j < 7))
    def _():
        kv01_ref[0] = acc

    @pl.when(j >= 7)
    def _():
        kv2_ref[...] = acc


def _qkv_wcol(j):
    jj = jnp.maximum(j - 3, 0)
    return jnp.where(j < 3, j, (1 + jj % 2) * 3 + jj // 2)


def _qkv(x, gain, w, tm):
    m = x.shape[0]
    tn = ATTN_WIDTH

    def kv01_idx(i, j):
        jj = jnp.clip(j - 3, 0, 3)
        return (jj // 2, i, jj % 2)

    est = tm * D_MODEL * 4 + 2 * (D_MODEL * tn * 2 + 3 * tm * tn * 4) + tm * D_MODEL * 2
    return pl.pallas_call(
        _qkv_kernel,
        grid=(m // tm, 9),
        in_specs=[
            pl.BlockSpec((tm, D_MODEL), lambda i, j: (i, 0), pipeline_mode=pl.Buffered(1)),
            pl.BlockSpec((1, D_MODEL), lambda i, j: (0, 0)),
            pl.BlockSpec((D_MODEL, tn), lambda i, j: (0, _qkv_wcol(j))),
        ],
        out_specs=[
            pl.BlockSpec((1, tm, tn), lambda i, j: (jnp.minimum(j, 2), i, 0)),
            pl.BlockSpec((1, tm, tn), kv01_idx),
            pl.BlockSpec((tm, tn), lambda i, j: (i, jnp.clip(j - 7, 0, 1))),
        ],
        out_shape=[
            jax.ShapeDtypeStruct((3, m, tn), F32),
            jax.ShapeDtypeStruct((2, m, 2 * tn), F32),
            jax.ShapeDtypeStruct((m, 2 * tn), F32),
        ],
        scratch_shapes=[pltpu.VMEM((tm, D_MODEL), BF16)],
        compiler_params=_params(("parallel", "arbitrary"), est),
        name="rms_qkv",
    )(x, gain.reshape(1, D_MODEL), w)


QB = 128


def _attn_bias_table():
    steps = (QB + np.arange(QB)[:, None] - np.arange(2 * QB)[None, :]).astype(np.float32)
    valid = (steps >= 0) & (steps <= N_STEPS)
    bias = -_head_slopes()[:, None, None] * steps[None]
    return jnp.asarray(np.where(valid[None], bias, NEG).astype(np.float32))


def _attn_prompt_kernel(q0_ref, q1_ref, q2_ref, k0_ref, v0_ref, k1_ref, v1_ref, k2_ref, v2_ref, bias_ref, o_ref,
                        og0_ref, og1_ref, og2_ref, lg0_ref, lg1_ref, lg2_ref):
    nt = (((1,), (1,)), ((), ()))
    groups = ((q0_ref, k0_ref, v0_ref, og0_ref, lg0_ref), (q1_ref, k1_ref, v1_ref, og1_ref, lg1_ref),
              (q2_ref, k2_ref, v2_ref, og2_ref, lg2_ref))
    for g, (q_ref, k_ref, v_ref, og_ref, lg_ref) in enumerate(groups):
        dil = DILATIONS[g]

        def rows(first_pos, n, r):
            start = r + dil * first_pos
            return pl.ds(start, n) if dil == 1 else pl.ds(start, n, stride=dil)

        for r in range(dil):
            for blk in range(SEQ // dil // QB):
                qrows = rows(blk * QB, QB, r)
                if blk == 0:
                    krows, bias = qrows, bias_ref[:, QB:]
                else:
                    krows, bias = rows((blk - 1) * QB, 2 * QB, r), bias_ref[...]
                s = lax.dot_general(q_ref[qrows, :].astype(BF16), k_ref[krows, :].astype(BF16), nt,
                                    preferred_element_type=F32)
                s = s * ATTN_SCALE + bias
                m = jnp.max(s, axis=-1, keepdims=True)
                p = jnp.exp(s - m)
                l = jnp.sum(p, axis=-1, keepdims=True)
                o = jnp.dot(p.astype(BF16), v_ref[krows, :].astype(BF16), preferred_element_type=F32)
                og_ref[qrows, :] = o * (1.0 / l)
                lg_ref[qrows, :] = jnp.broadcast_to(m + jnp.log(l), (QB, HEAD_DIM))

    for blk in range(SEQ // QB):
        rs = slice(blk * QB, (blk + 1) * QB)
        l0, l1, l2 = lg0_ref[rs, :], lg1_ref[rs, :], lg2_ref[rs, :]
        mx = jnp.maximum(jnp.maximum(l0, l1), l2)
        e0, e1, e2 = jnp.exp(l0 - mx), jnp.exp(l1 - mx), jnp.exp(l2 - mx)
        merged = (e0 * og0_ref[rs, :] + e1 * og1_ref[rs, :] + e2 * og2_ref[rs, :]) * (1.0 / (e0 + e1 + e2))
        o_ref[rs, :] = merged.astype(BF16)


def _attn_prompt(q3, kv01, kv2):
    bias = _attn_bias_table()
    qv = q3.reshape(3, BATCH, SEQ, ATTN_WIDTH)
    kv01v = kv01.reshape(2, BATCH, SEQ, KV_W)
    kv2v = kv2.reshape(BATCH, SEQ, KV_W)
    head = lambda g, off: pl.BlockSpec((None, None, SEQ, HEAD_DIM), lambda b, h: (g, b, 0, off + h))
    head2 = lambda off: pl.BlockSpec((None, SEQ, HEAD_DIM), lambda b, h: (b, 0, off + h))
    slab = SEQ * HEAD_DIM * 4
    est = 2 * (9 * slab + QB * 2 * QB * 4 + slab // 2) + 6 * slab
    o = pl.pallas_call(
        _attn_prompt_kernel,
        grid=(BATCH, N_HEADS),
        in_specs=[head(0, 0), head(1, 0), head(2, 0),
                  head(0, 0), head(0, N_HEADS), head(1, 0), head(1, N_HEADS), head2(0), head2(N_HEADS),
                  pl.BlockSpec((None, QB, 2 * QB), lambda b, h: (h, 0, 0))],
        out_specs=pl.BlockSpec((None, SEQ, HEAD_DIM), lambda b, h: (b, 0, h)),
        out_shape=jax.ShapeDtypeStruct((BATCH, SEQ, ATTN_WIDTH), BF16),
        scratch_shapes=[pltpu.VMEM((SEQ, HEAD_DIM), F32)] * 6,
        compiler_params=_params(("parallel", "arbitrary"), est),
        name="attn_prompt",
    )(qv, qv, qv, kv01v, kv01v, kv01v, kv01v, kv2v, kv2v, bias)
    return o.reshape(MP, ATTN_WIDTH)


def _wo_kernel(o_ref, x_ref, w_ref, out_ref):
    out_ref[...] = x_ref[...] + jnp.dot(o_ref[...].astype(BF16), w_ref[...], preferred_element_type=F32)


def _wo(o, x, w, tm):
    m = x.shape[0]
    est = 2 * (tm * ATTN_WIDTH * 4 + 2 * tm * D_MODEL * 4 + ATTN_WIDTH * D_MODEL * 2)
    row = lambda i: (i, 0)
    return pl.pallas_call(
        _wo_kernel,
        grid=(m // tm,),
        in_specs=[pl.BlockSpec((tm, ATTN_WIDTH), row), pl.BlockSpec((tm, D_MODEL), row),
                  pl.BlockSpec((ATTN_WIDTH, D_MODEL), lambda i: (0, 0))],
        out_specs=pl.BlockSpec((tm, D_MODEL), row),
        out_shape=jax.ShapeDtypeStruct((m, D_MODEL), F32),
        compiler_params=_params(("parallel",), est),
        name="wo_residual",
    )(o, x, w)


S_COLS = DEC_SEQ * N_HEADS
SAMPLE_BB = 2
G0_TOKENS = WINDOWS[0]
GD_TOKENS = DEC_SEQ * N_STEPS


def _attn_sample_kernel(q_ref, kvn_ref, c0_ref, c1_ref, c2_ref, t0_ref, td_ref, tn_ref, e_ref, o_ref):
    nt = (((1,), (1,)), ((), ()))
    e_all = e_ref[DEC_SEQ]
    sub = lax.broadcasted_iota(jnp.int32, (N_HEADS, S_COLS), 0)
    col = lax.broadcasted_iota(jnp.int32, (N_HEADS, S_COLS), 1)
    tabs = (t0_ref, td_ref, td_ref)
    n_tok = (G0_TOKENS, GD_TOKENS, GD_TOKENS)

    def cached(g, bi, part):
        if g == 0:
            x = c0_ref[bi, :, part, :, :]
        elif g == 1:
            x = c1_ref[bi, :, part, :, :]
        else:
            x = c2_ref[bi, :, :, part, :, :].reshape(GD_TOKENS, N_HEADS, HEAD_DIM)
        return x.reshape(n_tok[g] * N_HEADS, HEAD_DIM)

    for bi in range(SAMPLE_BB):
        tok = slice(bi * DEC_SEQ, (bi + 1) * DEC_SEQ)
        s_buf, s_new = [], []
        m_all = None
        for g in range(3):
            q = q_ref[g, bi * S_COLS:(bi + 1) * S_COLS, :]
            s = lax.dot_general(cached(g, bi, 0).astype(BF16), q, nt, preferred_element_type=F32)
            s = (s * ATTN_SCALE + tabs[g][...]).reshape(n_tok[g], N_HEADS, S_COLS)
            kn = kvn_ref[g, tok, 0, :, :].reshape(S_COLS, HEAD_DIM)
            sn = lax.dot_general(kn.astype(BF16), q, nt, preferred_element_type=F32)
            sn = (sn * ATTN_SCALE + tn_ref[min(g, 1)]).reshape(DEC_SEQ, N_HEADS, S_COLS)
            s_buf.append(s)
            s_new.append(sn)
            cm = jnp.maximum(jnp.max(s, axis=0), jnp.max(sn, axis=0))
            m_all = cm if m_all is None else jnp.maximum(m_all, cm)

        m_all = jnp.where(col % N_HEADS == sub, m_all, 0.0)
        l_all = jnp.zeros((N_HEADS, S_COLS), F32)
        acc = [jnp.zeros((N_HEADS, HEAD_DIM), F32) for _ in range(DEC_SEQ)]
        for g in range(3):
            p = jnp.exp(s_buf[g] - m_all[None])
            pn = jnp.exp(s_new[g] - m_all[None])
            l_all = l_all + jnp.sum(p, axis=0) + jnp.sum(pn, axis=0)
            pb = p.reshape(n_tok[g] * N_HEADS, S_COLS).astype(BF16)
            pnb = pn.reshape(S_COLS, S_COLS).astype(BF16)
            v = cached(g, bi, 1)
            vn = kvn_ref[g, tok, 1, :, :].reshape(S_COLS, HEAD_DIM)
            if g == 0:
                for t in range(DEC_SEQ):
                    w = jnp.dot(pb, e_ref[t], preferred_element_type=F32) * v
                    wn = jnp.dot(pnb, e_ref[t], preferred_element_type=F32) * vn
                    acc[t] = (acc[t] + jnp.sum(w.reshape(G0_TOKENS, N_HEADS, HEAD_DIM), axis=0)
                              + jnp.sum(wn.reshape(DEC_SEQ, N_HEADS, HEAD_DIM), axis=0))
            else:
                w = jnp.dot(pb, e_all, preferred_element_type=F32) * v
                wn = jnp.dot(pnb, e_all, preferred_element_type=F32) * vn
                wq = jnp.sum(w.reshape(N_STEPS, DEC_SEQ, N_HEADS, HEAD_DIM), axis=0)
                wnq = wn.reshape(DEC_SEQ, N_HEADS, HEAD_DIM)
                for t in range(DEC_SEQ):
                    acc[t] = acc[t] + wq[t] + wnq[t]

        inv = 1.0 / l_all
        for t in range(DEC_SEQ):
            inv_t = jnp.sum(jnp.where(col == t * N_HEADS + sub, inv, 0.0), axis=-1, keepdims=True)
            o_ref[bi * DEC_SEQ + t] = acc[t] * inv_t


def _sample_tables():
    base = _head_slopes()
    cols = np.arange(S_COLS)
    ct, ch = (cols // N_HEADS)[None, :], (cols % N_HEADS)[None, :]
    slope = base[cols % N_HEADS][None, :]

    def rows(n_tokens):
        r = np.arange(n_tokens * N_HEADS)[:, None]
        return r // N_HEADS, r % N_HEADS

    m, h = rows(G0_TOKENS)
    t0 = np.where((h == ch) & (m >= ct), -slope * (G0_TOKENS + ct - m), NEG)
    j, h = rows(DEC_SEQ)
    tn0 = np.where((h == ch) & (j <= ct), -slope * (ct - j), NEG)
    tnd = np.where((h == ch) & (j == ct), 0.0, NEG)
    n, h = rows(GD_TOKENS)
    td = np.where((h == ch) & (n % DEC_SEQ == ct), -slope * (N_STEPS - n // DEC_SEQ), NEG)
    expand = np.zeros((DEC_SEQ + 1, S_COLS, HEAD_DIM), np.float32)
    for t in range(DEC_SEQ):
        expand[t, t * N_HEADS:(t + 1) * N_HEADS, :] = 1.0
    expand[DEC_SEQ] = 1.0
    f = lambda a: jnp.asarray(a.astype(np.float32))
    return f(t0), f(td), f(np.stack([tn0, tnd])), jnp.asarray(expand, dtype=BF16)


def _attn_sample(q3, kv01, kv2, cache0, cache1, cache2):
    t0, td, tn, expand = _sample_tables()
    kvn = jnp.concatenate([kv01, kv2[None]], axis=0).reshape(3, MS, 2, N_HEADS, HEAD_DIM)
    q = q3.astype(BF16).reshape(3, MS * N_HEADS, HEAD_DIM)
    c2v = cache2.reshape(1, DEC_BATCH, N_STEPS, DILATIONS[2], 2, N_HEADS, HEAD_DIM)
    bb = SAMPLE_BB
    est = 2 * bb * (G0_TOKENS + 2 * GD_TOKENS) * KV_W * 4 + 16 * 1024 * 1024
    full = lambda a: pl.BlockSpec(a.shape, lambda i: (0,) * a.ndim)
    o = pl.pallas_call(
        _attn_sample_kernel,
        grid=(DEC_BATCH // bb,),
        in_specs=[
            pl.BlockSpec((3, bb * S_COLS, HEAD_DIM), lambda i: (0, i, 0)),
            pl.BlockSpec((3, bb * DEC_SEQ, 2, N_HEADS, HEAD_DIM), lambda i: (0, i, 0, 0, 0)),
            pl.BlockSpec((None, bb, G0_TOKENS, 2, N_HEADS, HEAD_DIM), lambda i: (0, i, 0, 0, 0, 0)),
            pl.BlockSpec((None, bb, GD_TOKENS, 2, N_HEADS, HEAD_DIM), lambda i: (0, i, 0, 0, 0, 0)),
            pl.BlockSpec((None, bb, N_STEPS, DEC_SEQ, 2, N_HEADS, HEAD_DIM), lambda i: (0, i, 0, 0, 0, 0, 0)),
            full(t0), full(td), full(tn), full(expand),
        ],
        out_specs=pl.BlockSpec((bb * DEC_SEQ, N_HEADS, HEAD_DIM), lambda i: (i, 0, 0)),
        out_shape=jax.ShapeDtypeStruct((MS, N_HEADS, HEAD_DIM), F32),
        compiler_params=_params(("parallel",), est),
        name="attn_sample",
    )(q, kvn, cache0, cache1, c2v, t0, td, tn, expand)
    return o.reshape(MS, ATTN_WIDTH)


def _mlp_kernel(x_ref, g_ref, wu_ref, wd_ref, gf_ref, out_ref, xn_ref, *, final):
    f = pl.program_id(1)

    @pl.when(f == 0)
    def _():
        xn_ref[...] = _rms(x_ref[...], g_ref[...]).astype(BF16)
        out_ref[...] = jnp.zeros_like(out_ref)

    h = jnp.dot(xn_ref[...], wu_ref[...], preferred_element_type=F32)
    h = jnp.maximum(h, 0.0)
    h = (h * h).astype(BF16)
    out_ref[...] += jnp.dot(h, wd_ref[...], preferred_element_type=F32)

    @pl.when(f == pl.num_programs(1) - 1)
    def _():
        y = x_ref[...] + out_ref[...]
        if final:
            y = _rms(y, gf_ref[...])
        out_ref[...] = y


def _mlp(x, gain, w_up, w_down, final_gain, tm, tf, final):
    m = x.shape[0]
    once = pl.Buffered(1)
    est = 2 * tm * D_MODEL * 4 + 2 * (2 * D_MODEL * tf * 2) + tm * D_MODEL * 2 + tm * tf * 6
    return pl.pallas_call(
        functools.partial(_mlp_kernel, final=final),
        grid=(m // tm, D_FF // tf),
        in_specs=[
            pl.BlockSpec((tm, D_MODEL), lambda i, f: (i, 0), pipeline_mode=once),
            pl.BlockSpec((1, D_MODEL), lambda i, f: (0, 0)),
            pl.BlockSpec((D_MODEL, tf), lambda i, f: (0, f)),
            pl.BlockSpec((tf, D_MODEL), lambda i, f: (f, 0)),
            pl.BlockSpec((1, D_MODEL), lambda i, f: (0, 0)),
        ],
        out_specs=pl.BlockSpec((tm, D_MODEL), lambda i, f: (i, 0), pipeline_mode=once),
        out_shape=jax.ShapeDtypeStruct((m, D_MODEL), F32),
        scratch_shapes=[pltpu.VMEM((tm, D_MODEL), BF16)],
        compiler_params=_params(("parallel", "arbitrary"), est),
        name="mlp_final" if final else "mlp",
    )(x, gain.reshape(1, D_MODEL), w_up, w_down, final_gain.reshape(1, D_MODEL))


def _pw1_kernel(x_ref, g_ref, wa_ref, wb_ref, ba_ref, bb_ref, u_ref, xn_ref):
    @pl.when(pl.program_id(1) == 0)
    def _():
        xn_ref[...] = _rms(x_ref[...], g_ref[...]).astype(BF16)

    xn = xn_ref[...]
    a = jnp.dot(xn, wa_ref[...], preferred_element_type=F32) + ba_ref[...]
    b = jnp.dot(xn, wb_ref[...], preferred_element_type=F32) + bb_ref[...]
    u_ref[...] = a * (1.0 / (1.0 + jnp.exp(-b)))


def _pw1_glu(x, gain, w, bias, tm, tn):
    m = x.shape[0]
    nj = D_MODEL // tn
    b2 = bias.reshape(1, 2 * D_MODEL)
    est = 2 * (tm * D_MODEL * 4 + 2 * D_MODEL * tn * 2 + tm * tn * 4) + tm * D_MODEL * 2 + 2 * tm * tn * 4
    return pl.pallas_call(
        _pw1_kernel,
        grid=(m // tm, nj),
        in_specs=[
            pl.BlockSpec((tm, D_MODEL), lambda i, j: (i, 0)),
            pl.BlockSpec((1, D_MODEL), lambda i, j: (0, 0)),
            pl.BlockSpec((D_MODEL, tn), lambda i, j: (0, j)),
            pl.BlockSpec((D_MODEL, tn), lambda i, j: (0, nj + j)),
            pl.BlockSpec((1, tn), lambda i, j: (0, j)),
            pl.BlockSpec((1, tn), lambda i, j: (0, nj + j)),
        ],
        out_specs=pl.BlockSpec((tm, tn), lambda i, j: (i, j)),
        out_shape=jax.ShapeDtypeStruct((m, D_MODEL), F32),
        scratch_shapes=[pltpu.VMEM((tm, D_MODEL), BF16)],
        compiler_params=_params(("parallel", "arbitrary"), est),
        name="pw1_glu",
    )(x, gain.reshape(1, D_MODEL), w, w, b2, b2)


CONV_TS = 256
CONV_HALO = 32


def _dw_prompt_kernel(prev_ref, u_ref, w_ref, b_ref, z_ref, ext_ref, sh_ref):
    i = pl.program_id(1)

    @pl.when(i == 0)
    def _():
        ext_ref[0:CONV_HALO, :] = jnp.zeros((CONV_HALO, D_MODEL), F32)

    @pl.when(i > 0)
    def _():
        ext_ref[0:CONV_HALO, :] = prev_ref[...]

    ext_ref[CONV_HALO:, :] = u_ref[...]
    off = CONV_HALO - HIST
    span = CONV_TS + CONV_HALO - 8

    def chunk(c, carry):
        cs = pl.ds(pl.multiple_of(c * 128, 128), 128)
        for phase in range(1, 8):
            sh_ref[phase - 1] = ext_ref[pl.ds(phase, span), cs]
        acc = jnp.zeros((CONV_TS, 128), F32) + b_ref[:, cs]
        for j in range(CONV_WIDTH):
            a, phase = divmod(j + off, 8)
            if phase == 0:
                src = ext_ref[pl.ds(8 * a, CONV_TS), cs]
            else:
                src = sh_ref[phase - 1, pl.ds(8 * a, CONV_TS), :]
            acc = acc + src * w_ref[pl.ds(j, 1), cs]
        z_ref[:, cs] = acc
        return carry

    lax.fori_loop(0, D_MODEL // 128, chunk, 0)


def _dw_prompt(u, w_dw, b_dw):
    u3 = u.reshape(BATCH, SEQ, D_MODEL)
    nt = SEQ // CONV_TS
    per = CONV_TS // CONV_HALO
    est = 2 * (CONV_HALO + 2 * CONV_TS) * D_MODEL * 4 + (CONV_TS + CONV_HALO) * D_MODEL * 4 + 2 * 32 * D_MODEL * 4
    z = pl.pallas_call(
        _dw_prompt_kernel,
        grid=(BATCH, nt),
        in_specs=[
            pl.BlockSpec((None, CONV_HALO, D_MODEL), lambda b, i: (b, jnp.maximum(i * per - 1, 0), 0)),
            pl.BlockSpec((None, CONV_TS, D_MODEL), lambda b, i: (b, i, 0)),
            pl.BlockSpec((CONV_WIDTH, D_MODEL), lambda b, i: (0, 0)),
            pl.BlockSpec((1, D_MODEL), lambda b, i: (0, 0)),
        ],
        out_specs=pl.BlockSpec((None, CONV_TS, D_MODEL), lambda b, i: (b, i, 0)),
        out_shape=jax.ShapeDtypeStruct((BATCH, SEQ, D_MODEL), F32),
        scratch_shapes=[pltpu.VMEM((CONV_TS + CONV_HALO, D_MODEL), F32),
                        pltpu.VMEM((7, CONV_TS + CONV_HALO - 8, 128), F32)],
        compiler_params=_params(("parallel", "arbitrary"), est),
        name="dwconv_prompt",
    )(u3, u3, w_dw, b_dw.reshape(1, D_MODEL))
    return z.reshape(MP, D_MODEL)


DW_BB = 8


def _dw_sample_kernel(hist_ref, u_ref, wh_ref, wu_ref, b_ref, z_ref):
    for b in range(DW_BB):
        hist = hist_ref[b]
        u = u_ref[b]
        for t in range(DEC_SEQ):
            z = jnp.sum(wh_ref[t] * hist, axis=0, keepdims=True)
            z = z + jnp.sum(wu_ref[t] * u, axis=0, keepdims=True)
            z_ref[b, t:t + 1, :] = z + b_ref[...]


def _dw_sample(hist, u, w_dw, b_dw):
    zeros = jnp.zeros((DEC_SEQ, D_MODEL), F32)
    wh = jnp.stack([jnp.concatenate([zeros[:t], w_dw[:HIST - t]], axis=0) for t in range(DEC_SEQ)])
    wu = jnp.stack([jnp.concatenate([w_dw[HIST - t:], zeros[:DEC_SEQ - 1 - t]], axis=0) for t in range(DEC_SEQ)])
    u3 = u.reshape(DEC_BATCH, DEC_SEQ, D_MODEL)
    est = 4 * (DW_BB * (32 + 8 + 8) * D_MODEL * 4) + 2 * DEC_SEQ * 40 * D_MODEL * 4
    z = pl.pallas_call(
        _dw_sample_kernel,
        grid=(DEC_BATCH // DW_BB,),
        in_specs=[
            pl.BlockSpec((DW_BB, HIST, D_MODEL), lambda i: (i, 0, 0)),
            pl.BlockSpec((DW_BB, DEC_SEQ, D_MODEL), lambda i: (i, 0, 0)),
            pl.BlockSpec((DEC_SEQ, HIST, D_MODEL), lambda i: (0, 0, 0)),
            pl.BlockSpec((DEC_SEQ, DEC_SEQ, D_MODEL), lambda i: (0, 0, 0)),
            pl.BlockSpec((1, D_MODEL), lambda i: (0, 0)),
        ],
        out_specs=pl.BlockSpec((DW_BB, DEC_SEQ, D_MODEL), lambda i: (i, 0, 0)),
        out_shape=jax.ShapeDtypeStruct((DEC_BATCH, DEC_SEQ, D_MODEL), F32),
        compiler_params=_params(("parallel",), est),
        name="dwconv_sample",
    )(hist, u3, wh, wu, b_dw.reshape(1, D_MODEL))
    return z.reshape(MS, D_MODEL)


def _ln_pw2_kernel(z_ref, x_ref, lg_ref, lb_ref, w_ref, b_ref, out_ref):
    z = z_ref[...]
    mu = jnp.mean(z, axis=-1, keepdims=True)
    zc = z - mu
    var = jnp.mean(zc * zc, axis=-1, keepdims=True)
    y = zc * lax.rsqrt(var + LN_EPS) * lg_ref[...] + lb_ref[...]
    y = y * (1.0 / (1.0 + jnp.exp(-y)))
    out_ref[...] = x_ref[...] + jnp.dot(y.astype(BF16), w_ref[...], preferred_element_type=F32) + b_ref[...]


def _ln_pw2(z, x, ln_g, ln_b, w, bias, tm):
    m = x.shape[0]
    est = 2 * (3 * tm * D_MODEL * 4 + D_MODEL * D_MODEL * 2) + 2 * tm * D_MODEL * 4
    row = lambda i: (i, 0)
    vec = pl.BlockSpec((1, D_MODEL), lambda i: (0, 0))
    return pl.pallas_call(
        _ln_pw2_kernel,
        grid=(m // tm,),
        in_specs=[pl.BlockSpec((tm, D_MODEL), row), pl.BlockSpec((tm, D_MODEL), row), vec, vec,
                  pl.BlockSpec((D_MODEL, D_MODEL), lambda i: (0, 0)), vec],
        out_specs=pl.BlockSpec((tm, D_MODEL), row),
        out_shape=jax.ShapeDtypeStruct((m, D_MODEL), F32),
        compiler_params=_params(("parallel",), est),
        name="ln_swish_pw2",
    )(z, x, ln_g.reshape(1, D_MODEL), ln_b.reshape(1, D_MODEL), w, bias.reshape(1, D_MODEL))


def kernel(x_prompt, x_sample, cache_kv_g0, cache_kv_g1, cache_kv_g2, state_conv, attn_norm, w_qkv, w_o,
           conv_norm, w_pw1, b_pw1, w_dw, b_dw, conv_ln_g, conv_ln_b, w_pw2, b_pw2, mlp_norm, w_up, w_down,
           final_norm):
    xp = x_prompt.reshape(MP, D_MODEL)
    xs = x_sample.reshape(MS, D_MODEL)

    wqkv = w_qkv[0].astype(BF16)
    wo = w_o[0].astype(BF16)
    q3p, kv01p, kv2p = _qkv(xp, attn_norm[0], wqkv, tm=1024)
    q3s, kv01s, kv2s = _qkv(xs, attn_norm[0], wqkv, tm=MS)

    o_p = _attn_prompt(q3p, kv01p, kv2p)
    xp = _wo(o_p, xp, wo, tm=512)

    o_s = _attn_sample(q3s, kv01s, kv2s, cache_kv_g0, cache_kv_g1, cache_kv_g2)
    xs = _wo(o_s, xs, wo, tm=MS)

    wu0, wd0 = w_up[0].astype(BF16), w_down[0].astype(BF16)
    xp = _mlp(xp, mlp_norm[0], wu0, wd0, final_norm, tm=1024, tf=1024, final=False)
    xs = _mlp(xs, mlp_norm[0], wu0, wd0, final_norm, tm=MS, tf=1024, final=False)

    w1 = w_pw1[0].astype(BF16)
    w2 = w_pw2[0].astype(BF16)
    up = _pw1_glu(xp, conv_norm[0], w1, b_pw1[0], tm=1024, tn=512)
    us = _pw1_glu(xs, conv_norm[0], w1, b_pw1[0], tm=MS, tn=512)
    zp = _dw_prompt(up, w_dw[0], b_dw[0])
    zs = _dw_sample(state_conv[0], us, w_dw[0], b_dw[0])
    xp = _ln_pw2(zp, xp, conv_ln_g[0], conv_ln_b[0], w2, b_pw2[0], tm=512)
    xs = _ln_pw2(zs, xs, conv_ln_g[0], conv_ln_b[0], w2, b_pw2[0], tm=MS)

    wu1, wd1 = w_up[1].astype(BF16), w_down[1].astype(BF16)
    yp = _mlp(xp, mlp_norm[1], wu1, wd1, final_norm, tm=1024, tf=1024, final=True)
    ys = _mlp(xs, mlp_norm[1], wu1, wd1, final_norm, tm=MS, tf=1024, final=True)

    def tail(kv, idx, keep):
        view = kv.reshape((-1, BATCH, SEQ, KV_W))
        return view[idx:idx + 1, :, SEQ - keep:].reshape(1, BATCH, keep, 2, N_HEADS, HEAD_DIM)

    def new_rows(kv):
        return kv.reshape(1, DEC_BATCH, DEC_SEQ, 2, N_HEADS, HEAD_DIM)

    conv_prompt = up.reshape(BATCH, SEQ, D_MODEL)[:, SEQ - HIST:][None]
    conv_sample = jnp.concatenate([state_conv[0][:, DEC_SEQ:], us.reshape(DEC_BATCH, DEC_SEQ, D_MODEL)], axis=1)[None]
    return (yp.reshape(BATCH, SEQ, D_MODEL), ys.reshape(DEC_BATCH, DEC_SEQ, D_MODEL),
            tail(kv01p, 0, WINDOWS[0]), tail(kv01p, 1, WINDOWS[1]), tail(kv2p, 0, WINDOWS[2]),
            conv_prompt,
            new_rows(kv01s[0]), new_rows(kv01s[1]), new_rows(kv2s),
            conv_sample)
```

```python
import functools

import numpy as np
import jax
import jax.numpy as jnp
from jax import lax
from jax.experimental import pallas as pl
from jax.experimental.pallas import tpu as pltpu

F32 = jnp.float32
BF16 = jnp.bfloat16

D_MODEL = 2048
BATCH = 8
SEQ = 2048
DEC_BATCH = 128
DEC_SEQ = 4
HEAD_DIM = 128
N_HEADS = 8
ATTN_WIDTH = N_HEADS * HEAD_DIM
KV_W = 2 * ATTN_WIDTH
WINDOWS = (128, 512, 2048)
DILATIONS = (1, 4, 16)
N_STEPS = 128
ATTN_SCALE = HEAD_DIM ** -0.5
CONV_WIDTH = 31
HIST = CONV_WIDTH - 1
D_FF = 4 * D_MODEL
RMS_EPS = 1e-6
LN_EPS = 1e-5
NEG = -1e30

MP = BATCH * SEQ
MS = DEC_BATCH * DEC_SEQ

V7X_VMEM_BYTES = 64 * 1024 * 1024
VMEM_HEADROOM_BYTES = 16 * 1024 * 1024


def _params(semantics, vmem_estimate):
    limit = min(int(vmem_estimate) + VMEM_HEADROOM_BYTES, V7X_VMEM_BYTES - 4 * 1024 * 1024)
    return pltpu.CompilerParams(dimension_semantics=semantics, vmem_limit_bytes=limit)


def _rms(x, g):
    y = x * lax.rsqrt(jnp.mean(x * x, axis=-1, keepdims=True) + RMS_EPS)
    return y * g


def _head_slopes():
    return (2.0 ** (-8.0 * np.arange(1, N_HEADS + 1) / N_HEADS)).astype(np.float32)


def _qkv_kernel(x_ref, g_ref, w_ref, q_ref, kv01_ref, kv2_ref, xn_ref):
    j = pl.program_id(1)

    @pl.when(j == 0)
    def _():
        xn_ref[...] = _rms(x_ref[...], g_ref[...]).astype(BF16)

    acc = jnp.dot(xn_ref[...], w_ref[...], preferred_element_type=F32)

    @pl.when(j < 3)
    def _():
        q_ref[0] = acc.astype(BF16)

    @pl.when(jnp.logical_and(j >= 3, j < 7))
    def _():
        kv01_ref[0] = acc

    @pl.when(j >= 7)
    def _():
        kv2_ref[...] = acc


def _qkv_wcol(j):
    jj = jnp.maximum(j - 3, 0)
    return jnp.where(j < 3, j, (1 + jj % 2) * 3 + jj // 2)


def _qkv(x, gain, w, tm):
    m = x.shape[0]
    tn = ATTN_WIDTH

    def kv01_idx(i, j):
        jj = jnp.clip(j - 3, 0, 3)
        return (jj // 2, i, jj % 2)

    est = tm * D_MODEL * 4 + 2 * (D_MODEL * tn * 2 + tm * tn * 2 + 2 * tm * tn * 4) + tm * D_MODEL * 2
    return pl.pallas_call(
        _qkv_kernel,
        grid=(m // tm, 9),
        in_specs=[
            pl.BlockSpec((tm, D_MODEL), lambda i, j: (i, 0), pipeline_mode=pl.Buffered(1)),
            pl.BlockSpec((1, D_MODEL), lambda i, j: (0, 0)),
            pl.BlockSpec((D_MODEL, tn), lambda i, j: (0, _qkv_wcol(j))),
        ],
        out_specs=[
            pl.BlockSpec((1, tm, tn), lambda i, j: (jnp.minimum(j, 2), i, 0)),
            pl.BlockSpec((1, tm, tn), kv01_idx),
            pl.BlockSpec((tm, tn), lambda i, j: (i, jnp.clip(j - 7, 0, 1))),
        ],
        out_shape=[
            jax.ShapeDtypeStruct((3, m, tn), BF16),
            jax.ShapeDtypeStruct((2, m, 2 * tn), F32),
            jax.ShapeDtypeStruct((m, 2 * tn), F32),
        ],
        scratch_shapes=[pltpu.VMEM((tm, D_MODEL), BF16)],
        compiler_params=_params(("parallel", "arbitrary"), est),
        name="rms_qkv",
    )(x, gain.reshape(1, D_MODEL), w)


QB = 128


def _attn_bias_table():
    steps = (QB + np.arange(QB)[:, None] - np.arange(2 * QB)[None, :]).astype(np.float32)
    valid = (steps >= 0) & (steps <= N_STEPS)
    bias = -_head_slopes()[:, None, None] * steps[None]
    return jnp.asarray(np.where(valid[None], bias, NEG).astype(np.float32))


def _attn_prompt_kernel(q0_ref, q1_ref, q2_ref, k0_ref, v0_ref, k1_ref, v1_ref, k2_ref, v2_ref, bias_ref, o_ref,
                        og0_ref, og1_ref, og2_ref, lg0_ref, lg1_ref, lg2_ref, qf_ref):
    nt = (((1,), (1,)), ((), ()))
    groups = ((q0_ref, k0_ref, v0_ref, og0_ref, lg0_ref), (q1_ref, k1_ref, v1_ref, og1_ref, lg1_ref),
              (q2_ref, k2_ref, v2_ref, og2_ref, lg2_ref))
    for g, (q_ref, k_ref, v_ref, og_ref, lg_ref) in enumerate(groups):
        dil = DILATIONS[g]
        if dil > 1:
            qf_ref[...] = q_ref[...].astype(F32)
            q_ref = qf_ref

        def rows(first_pos, n, r):
            start = r + dil * first_pos
            return pl.ds(start, n) if dil == 1 else pl.ds(start, n, stride=dil)

        for r in range(dil):
            for blk in range(SEQ // dil // QB):
                qrows = rows(blk * QB, QB, r)
                if blk == 0:
                    krows, bias = qrows, bias_ref[:, QB:]
                else:
                    krows, bias = rows((blk - 1) * QB, 2 * QB, r), bias_ref[...]
                s = lax.dot_general(q_ref[qrows, :].astype(BF16), k_ref[krows, :].astype(BF16), nt,
                                    preferred_element_type=F32)
                s = s * ATTN_SCALE + bias
                m = jnp.max(s, axis=-1, keepdims=True)
                p = jnp.exp(s - m)
                l = jnp.sum(p, axis=-1, keepdims=True)
                o = jnp.dot(p.astype(BF16), v_ref[krows, :].astype(BF16), preferred_element_type=F32)
                og_ref[qrows, :] = o * (1.0 / l)
                lg_ref[qrows, :] = jnp.broadcast_to(m + jnp.log(l), (QB, HEAD_DIM))

    for blk in range(SEQ // QB):
        rs = slice(blk * QB, (blk + 1) * QB)
        l0, l1, l2 = lg0_ref[rs, :], lg1_ref[rs, :], lg2_ref[rs, :]
        mx = jnp.maximum(jnp.maximum(l0, l1), l2)
        e0, e1, e2 = jnp.exp(l0 - mx), jnp.exp(l1 - mx), jnp.exp(l2 - mx)
        merged = (e0 * og0_ref[rs, :] + e1 * og1_ref[rs, :] + e2 * og2_ref[rs, :]) * (1.0 / (e0 + e1 + e2))
        o_ref[rs, :] = merged.astype(BF16)


def _attn_prompt(q3, kv01, kv2):
    bias = _attn_bias_table()
    qv = q3.reshape(3, BATCH, SEQ, ATTN_WIDTH)
    kv01v = kv01.reshape(2, BATCH, SEQ, KV_W)
    kv2v = kv2.reshape(BATCH, SEQ, KV_W)
    head = lambda g, off: pl.BlockSpec((None, None, SEQ, HEAD_DIM), lambda b, h: (g, b, 0, off + h))
    head2 = lambda off: pl.BlockSpec((None, SEQ, HEAD_DIM), lambda b, h: (b, 0, off + h))
    slab = SEQ * HEAD_DIM * 4
    est = 2 * (8 * slab + QB * 2 * QB * 4) + 7 * slab
    o = pl.pallas_call(
        _attn_prompt_kernel,
        grid=(BATCH, N_HEADS),
        in_specs=[head(0, 0), head(1, 0), head(2, 0),
                  head(0, 0), head(0, N_HEADS), head(1, 0), head(1, N_HEADS), head2(0), head2(N_HEADS),
                  pl.BlockSpec((None, QB, 2 * QB), lambda b, h: (h, 0, 0))],
        out_specs=pl.BlockSpec((None, SEQ, HEAD_DIM), lambda b, h: (b, 0, h)),
        out_shape=jax.ShapeDtypeStruct((BATCH, SEQ, ATTN_WIDTH), BF16),
        scratch_shapes=[pltpu.VMEM((SEQ, HEAD_DIM), F32)] * 7,
        compiler_params=_params(("parallel", "arbitrary"), est),
        name="attn_prompt",
    )(qv, qv, qv, kv01v, kv01v, kv01v, kv01v, kv2v, kv2v, bias)
    return o.reshape(MP, ATTN_WIDTH)


def _wo_kernel(o_ref, x_ref, w_ref, out_ref):
    out_ref[...] = x_ref[...] + jnp.dot(o_ref[...].astype(BF16), w_ref[...], preferred_element_type=F32)


def _wo(o, x, w, tm):
    m = x.shape[0]
    est = 2 * (tm * ATTN_WIDTH * 4 + 2 * tm * D_MODEL * 4 + ATTN_WIDTH * D_MODEL * 2)
    row = lambda i: (i, 0)
    return pl.pallas_call(
        _wo_kernel,
        grid=(m // tm,),
        in_specs=[pl.BlockSpec((tm, ATTN_WIDTH), row), pl.BlockSpec((tm, D_MODEL), row),
                  pl.BlockSpec((ATTN_WIDTH, D_MODEL), lambda i: (0, 0))],
        out_specs=pl.BlockSpec((tm, D_MODEL), row),
        out_shape=jax.ShapeDtypeStruct((m, D_MODEL), F32),
        compiler_params=_params(("parallel",), est),
        name="wo_residual",
    )(o, x, w)


S_COLS = DEC_SEQ * N_HEADS
SAMPLE_BB = 2
G0_TOKENS = WINDOWS[0]
GD_TOKENS = DEC_SEQ * N_STEPS


def _attn_sample_kernel(q_ref, kvn_ref, c0_ref, c1_ref, c2_ref, t0_ref, td_ref, tn_ref, e_ref, o_ref):
    nt = (((1,), (1,)), ((), ()))
    e_all = e_ref[DEC_SEQ]
    sub = lax.broadcasted_iota(jnp.int32, (N_HEADS, S_COLS), 0)
    col = lax.broadcasted_iota(jnp.int32, (N_HEADS, S_COLS), 1)
    tabs = (t0_ref, td_ref, td_ref)
    n_tok = (G0_TOKENS, GD_TOKENS, GD_TOKENS)

    def cached(g, bi, part):
        if g == 0:
            x = c0_ref[bi, :, part, :, :]
        elif g == 1:
            x = c1_ref[bi, :, part, :, :]
        else:
            x = c2_ref[bi, :, :, part, :, :].reshape(GD_TOKENS, N_HEADS, HEAD_DIM)
        return x.reshape(n_tok[g] * N_HEADS, HEAD_DIM)

    for bi in range(SAMPLE_BB):
        tok = slice(bi * DEC_SEQ, (bi + 1) * DEC_SEQ)
        s_buf, s_new = [], []
        m_all = None
        for g in range(3):
            q = q_ref[g, bi * S_COLS:(bi + 1) * S_COLS, :]
            s = lax.dot_general(cached(g, bi, 0).astype(BF16), q, nt, preferred_element_type=F32)
            s = (s * ATTN_SCALE + tabs[g][...]).reshape(n_tok[g], N_HEADS, S_COLS)
            kn = kvn_ref[g, tok, 0, :, :].reshape(S_COLS, HEAD_DIM)
            sn = lax.dot_general(kn.astype(BF16), q, nt, preferred_element_type=F32)
            sn = (sn * ATTN_SCALE + tn_ref[min(g, 1)]).reshape(DEC_SEQ, N_HEADS, S_COLS)
            s_buf.append(s)
            s_new.append(sn)
            cm = jnp.maximum(jnp.max(s, axis=0), jnp.max(sn, axis=0))
            m_all = cm if m_all is None else jnp.maximum(m_all, cm)

        m_all = jnp.where(col % N_HEADS == sub, m_all, 0.0)
        l_all = jnp.zeros((N_HEADS, S_COLS), F32)
        acc = [jnp.zeros((N_HEADS, HEAD_DIM), F32) for _ in range(DEC_SEQ)]
        for g in range(3):
            p = jnp.exp(s_buf[g] - m_all[None])
            pn = jnp.exp(s_new[g] - m_all[None])
            l_all = l_all + jnp.sum(p, axis=0) + jnp.sum(pn, axis=0)
            pb = p.reshape(n_tok[g] * N_HEADS, S_COLS).astype(BF16)
            pnb = pn.reshape(S_COLS, S_COLS).astype(BF16)
            v = cached(g, bi, 1)
            vn = kvn_ref[g, tok, 1, :, :].reshape(S_COLS, HEAD_DIM)
            if g == 0:
                for t in range(DEC_SEQ):
                    w = jnp.dot(pb, e_ref[t], preferred_element_type=F32) * v
                    wn = jnp.dot(pnb, e_ref[t], preferred_element_type=F32) * vn
                    acc[t] = (acc[t] + jnp.sum(w.reshape(G0_TOKENS, N_HEADS, HEAD_DIM), axis=0)
                              + jnp.sum(wn.reshape(DEC_SEQ, N_HEADS, HEAD_DIM), axis=0))
            else:
                w = jnp.dot(pb, e_all, preferred_element_type=F32) * v
                wn = jnp.dot(pnb, e_all, preferred_element_type=F32) * vn
                wq = jnp.sum(w.reshape(N_STEPS, DEC_SEQ, N_HEADS, HEAD_DIM), axis=0)
                wnq = wn.reshape(DEC_SEQ, N_HEADS, HEAD_DIM)
                for t in range(DEC_SEQ):
                    acc[t] = acc[t] + wq[t] + wnq[t]

        inv = 1.0 / l_all
        for t in range(DEC_SEQ):
            inv_t = jnp.sum(jnp.where(col == t * N_HEADS + sub, inv, 0.0), axis=-1, keepdims=True)
            o_ref[bi * DEC_SEQ + t] = acc[t] * inv_t


def _sample_tables():
    base = _head_slopes()
    cols = np.arange(S_COLS)
    ct, ch = (cols // N_HEADS)[None, :], (cols % N_HEADS)[None, :]
    slope = base[cols % N_HEADS][None, :]

    def rows(n_tokens):
        r = np.arange(n_tokens * N_HEADS)[:, None]
        return r // N_HEADS, r % N_HEADS

    m, h = rows(G0_TOKENS)
    t0 = np.where((h == ch) & (m >= ct), -slope * (G0_TOKENS + ct - m), NEG)
    j, h = rows(DEC_SEQ)
    tn0 = np.where((h == ch) & (j <= ct), -slope * (ct - j), NEG)
    tnd = np.where((h == ch) & (j == ct), 0.0, NEG)
    n, h = rows(GD_TOKENS)
    td = np.where((h == ch) & (n % DEC_SEQ == ct), -slope * (N_STEPS - n // DEC_SEQ), NEG)
    expand = np.zeros((DEC_SEQ + 1, S_COLS, HEAD_DIM), np.float32)
    for t in range(DEC_SEQ):
        expand[t, t * N_HEADS:(t + 1) * N_HEADS, :] = 1.0
    expand[DEC_SEQ] = 1.0
    f = lambda a: jnp.asarray(a.astype(np.float32))
    return f(t0), f(td), f(np.stack([tn0, tnd])), jnp.asarray(expand, dtype=BF16)


def _attn_sample(q3, kv01, kv2, cache0, cache1, cache2):
    t0, td, tn, expand = _sample_tables()
    kvn = jnp.concatenate([kv01, kv2[None]], axis=0).reshape(3, MS, 2, N_HEADS, HEAD_DIM)
    q = q3.astype(BF16).reshape(3, MS * N_HEADS, HEAD_DIM)
    c2v = cache2.reshape(1, DEC_BATCH, N_STEPS, DILATIONS[2], 2, N_HEADS, HEAD_DIM)
    bb = SAMPLE_BB
    est = 2 * bb * (G0_TOKENS + 2 * GD_TOKENS) * KV_W * 4 + 16 * 1024 * 1024
    full = lambda a: pl.BlockSpec(a.shape, lambda i: (0,) * a.ndim)
    o = pl.pallas_call(
        _attn_sample_kernel,
        grid=(DEC_BATCH // bb,),
        in_specs=[
            pl.BlockSpec((3, bb * S_COLS, HEAD_DIM), lambda i: (0, i, 0)),
            pl.BlockSpec((3, bb * DEC_SEQ, 2, N_HEADS, HEAD_DIM), lambda i: (0, i, 0, 0, 0)),
            pl.BlockSpec((None, bb, G0_TOKENS, 2, N_HEADS, HEAD_DIM), lambda i: (0, i, 0, 0, 0, 0)),
            pl.BlockSpec((None, bb, GD_TOKENS, 2, N_HEADS, HEAD_DIM), lambda i: (0, i, 0, 0, 0, 0)),
            pl.BlockSpec((None, bb, N_STEPS, DEC_SEQ, 2, N_HEADS, HEAD_DIM), lambda i: (0, i, 0, 0, 0, 0, 0)),
            full(t0), full(td), full(tn), full(expand),
        ],
        out_specs=pl.BlockSpec((bb * DEC_SEQ, N_HEADS, HEAD_DIM), lambda i: (i, 0, 0)),
        out_shape=jax.ShapeDtypeStruct((MS, N_HEADS, HEAD_DIM), F32),
        compiler_params=_params(("parallel",), est),
        name="attn_sample",
    )(q, kvn, cache0, cache1, c2v, t0, td, tn, expand)
    return o.reshape(MS, ATTN_WIDTH)


def _mlp_kernel(x_ref, g_ref, wu_ref, wd_ref, gf_ref, out_ref, xn_ref, *, final):
    f = pl.program_id(1)

    @pl.when(f == 0)
    def _():
        xn_ref[...] = _rms(x_ref[...], g_ref[...]).astype(BF16)
        out_ref[...] = jnp.zeros_like(out_ref)

    h = jnp.dot(xn_ref[...], wu_ref[...], preferred_element_type=F32)
    h = jnp.maximum(h, 0.0)
    h = (h * h).astype(BF16)
    out_ref[...] += jnp.dot(h, wd_ref[...], preferred_element_type=F32)

    @pl.when(f == pl.num_programs(1) - 1)
    def _():
        y = x_ref[...] + out_ref[...]
        if final:
            y = _rms(y, gf_ref[...])
        out_ref[...] = y


def _mlp(x, gain, w_up, w_down, layer, final_gain, tm, tf, final):
    m = x.shape[0]
    once = pl.Buffered(1)
    est = 2 * tm * D_MODEL * 4 + 2 * (2 * D_MODEL * tf * 2) + tm * D_MODEL * 2 + tm * tf * 6
    return pl.pallas_call(
        functools.partial(_mlp_kernel, final=final),
        grid=(m // tm, D_FF // tf),
        in_specs=[
            pl.BlockSpec((tm, D_MODEL), lambda i, f: (i, 0), pipeline_mode=once),
            pl.BlockSpec((1, D_MODEL), lambda i, f: (0, 0)),
            pl.BlockSpec((None, D_MODEL, tf), lambda i, f: (layer, 0, f)),
            pl.BlockSpec((None, tf, D_MODEL), lambda i, f: (layer, f, 0)),
            pl.BlockSpec((1, D_MODEL), lambda i, f: (0, 0)),
        ],
        out_specs=pl.BlockSpec((tm, D_MODEL), lambda i, f: (i, 0), pipeline_mode=once),
        out_shape=jax.ShapeDtypeStruct((m, D_MODEL), F32),
        scratch_shapes=[pltpu.VMEM((tm, D_MODEL), BF16)],
        compiler_params=_params(("parallel", "arbitrary"), est),
        name="mlp_final" if final else "mlp",
    )(x, gain.reshape(1, D_MODEL), w_up, w_down, final_gain.reshape(1, D_MODEL))


def _pw1_kernel(x_ref, g_ref, wa_ref, wb_ref, ba_ref, bb_ref, u_ref, xn_ref):
    @pl.when(pl.program_id(1) == 0)
    def _():
        xn_ref[...] = _rms(x_ref[...], g_ref[...]).astype(BF16)

    xn = xn_ref[...]
    a = jnp.dot(xn, wa_ref[...], preferred_element_type=F32) + ba_ref[...]
    b = jnp.dot(xn, wb_ref[...], preferred_element_type=F32) + bb_ref[...]
    u_ref[...] = a * (1.0 / (1.0 + jnp.exp(-b)))


def _pw1_glu(x, gain, w, bias, tm, tn):
    m = x.shape[0]
    nj = D_MODEL // tn
    b2 = bias.reshape(1, 2 * D_MODEL)
    est = 2 * (tm * D_MODEL * 4 + 2 * D_MODEL * tn * 2 + tm * tn * 4) + tm * D_MODEL * 2 + 2 * tm * tn * 4
    return pl.pallas_call(
        _pw1_kernel,
        grid=(m // tm, nj),
        in_specs=[
            pl.BlockSpec((tm, D_MODEL), lambda i, j: (i, 0)),
            pl.BlockSpec((1, D_MODEL), lambda i, j: (0, 0)),
            pl.BlockSpec((D_MODEL, tn), lambda i, j: (0, j)),
            pl.BlockSpec((D_MODEL, tn), lambda i, j: (0, nj + j)),
            pl.BlockSpec((1, tn), lambda i, j: (0, j)),
            pl.BlockSpec((1, tn), lambda i, j: (0, nj + j)),
        ],
        out_specs=pl.BlockSpec((tm, tn), lambda i, j: (i, j)),
        out_shape=jax.ShapeDtypeStruct((m, D_MODEL), F32),
        scratch_shapes=[pltpu.VMEM((tm, D_MODEL), BF16)],
        compiler_params=_params(("parallel", "arbitrary"), est),
        name="pw1_glu",
    )(x, gain.reshape(1, D_MODEL), w, w, b2, b2)


CONV_TS = 256
CONV_HALO = 32


def _dw_prompt_kernel(prev_ref, u_ref, w_ref, b_ref, z_ref, ext_ref, sh_ref):
    i = pl.program_id(1)

    @pl.when(i == 0)
    def _():
        ext_ref[0:CONV_HALO, :] = jnp.zeros((CONV_HALO, D_MODEL), F32)

    @pl.when(i > 0)
    def _():
        ext_ref[0:CONV_HALO, :] = prev_ref[...]

    ext_ref[CONV_HALO:, :] = u_ref[...]
    off = CONV_HALO - HIST
    span = CONV_TS + CONV_HALO - 8

    def chunk(c, carry):
        cs = pl.ds(pl.multiple_of(c * 128, 128), 128)
        for phase in range(1, 8):
            sh_ref[phase - 1] = ext_ref[pl.ds(phase, span), cs]
        acc = jnp.zeros((CONV_TS, 128), F32) + b_ref[:, cs]
        for j in range(CONV_WIDTH):
            a, phase = divmod(j + off, 8)
            if phase == 0:
                src = ext_ref[pl.ds(8 * a, CONV_TS), cs]
            else:
                src = sh_ref[phase - 1, pl.ds(8 * a, CONV_TS), :]
            acc = acc + src * w_ref[pl.ds(j, 1), cs]
        z_ref[:, cs] = acc
        return carry

    lax.fori_loop(0, D_MODEL // 128, chunk, 0)


def _dw_prompt(u, w_dw, b_dw):
    u3 = u.reshape(BATCH, SEQ, D_MODEL)
    nt = SEQ // CONV_TS
    per = CONV_TS // CONV_HALO
    est = 2 * (CONV_HALO + 2 * CONV_TS) * D_MODEL * 4 + (CONV_TS + CONV_HALO) * D_MODEL * 4 + 2 * 32 * D_MODEL * 4
    z = pl.pallas_call(
        _dw_prompt_kernel,
        grid=(BATCH, nt),
        in_specs=[
            pl.BlockSpec((None, CONV_HALO, D_MODEL), lambda b, i: (b, jnp.maximum(i * per - 1, 0), 0)),
            pl.BlockSpec((None, CONV_TS, D_MODEL), lambda b, i: (b, i, 0)),
            pl.BlockSpec((CONV_WIDTH, D_MODEL), lambda b, i: (0, 0)),
            pl.BlockSpec((1, D_MODEL), lambda b, i: (0, 0)),
        ],
        out_specs=pl.BlockSpec((None, CONV_TS, D_MODEL), lambda b, i: (b, i, 0)),
        out_shape=jax.ShapeDtypeStruct((BATCH, SEQ, D_MODEL), F32),
        scratch_shapes=[pltpu.VMEM((CONV_TS + CONV_HALO, D_MODEL), F32),
                        pltpu.VMEM((7, CONV_TS + CONV_HALO - 8, 128), F32)],
        compiler_params=_params(("parallel", "arbitrary"), est),
        name="dwconv_prompt",
    )(u3, u3, w_dw, b_dw.reshape(1, D_MODEL))
    return z.reshape(MP, D_MODEL)


DW_BB = 8


def _dw_sample_kernel(hist_ref, u_ref, wh_ref, wu_ref, b_ref, z_ref, new_ref):
    for b in range(DW_BB):
        hist = hist_ref[b]
        u = u_ref[b]
        for t in range(DEC_SEQ):
            z = jnp.sum(wh_ref[t] * hist, axis=0, keepdims=True)
            z = z + jnp.sum(wu_ref[t] * u, axis=0, keepdims=True)
            z_ref[b, t:t + 1, :] = z + b_ref[...]
        new_ref[b, 0:HIST - DEC_SEQ, :] = hist[DEC_SEQ:, :]
        new_ref[b, HIST - DEC_SEQ:, :] = u


def _dw_sample(state, u, w_dw, b_dw):
    zeros = jnp.zeros((DEC_SEQ, D_MODEL), F32)
    wh = jnp.stack([jnp.concatenate([zeros[:t], w_dw[:HIST - t]], axis=0) for t in range(DEC_SEQ)])
    wu = jnp.stack([jnp.concatenate([w_dw[HIST - t:], zeros[:DEC_SEQ - 1 - t]], axis=0) for t in range(DEC_SEQ)])
    u3 = u.reshape(DEC_BATCH, DEC_SEQ, D_MODEL)
    est = 4 * (DW_BB * (32 + 32 + 8 + 8) * D_MODEL * 4) + 2 * DEC_SEQ * 40 * D_MODEL * 4
    z, new_state = pl.pallas_call(
        _dw_sample_kernel,
        grid=(DEC_BATCH // DW_BB,),
        in_specs=[
            pl.BlockSpec((None, DW_BB, HIST, D_MODEL), lambda i: (0, i, 0, 0)),
            pl.BlockSpec((DW_BB, DEC_SEQ, D_MODEL), lambda i: (i, 0, 0)),
            pl.BlockSpec((DEC_SEQ, HIST, D_MODEL), lambda i: (0, 0, 0)),
            pl.BlockSpec((DEC_SEQ, DEC_SEQ, D_MODEL), lambda i: (0, 0, 0)),
            pl.BlockSpec((1, D_MODEL), lambda i: (0, 0)),
        ],
        out_specs=[pl.BlockSpec((DW_BB, DEC_SEQ, D_MODEL), lambda i: (i, 0, 0)),
                   pl.BlockSpec((None, DW_BB, HIST, D_MODEL), lambda i: (0, i, 0, 0))],
        out_shape=[jax.ShapeDtypeStruct((DEC_BATCH, DEC_SEQ, D_MODEL), F32),
                   jax.ShapeDtypeStruct((1, DEC_BATCH, HIST, D_MODEL), F32)],
        compiler_params=_params(("parallel",), est),
        name="dwconv_sample",
    )(state, u3, wh, wu, b_dw.reshape(1, D_MODEL))
    return z.reshape(MS, D_MODEL), new_state


def _ln_pw2_kernel(z_ref, x_ref, lg_ref, lb_ref, w_ref, b_ref, out_ref):
    z = z_ref[...]
    mu = jnp.mean(z, axis=-1, keepdims=True)
    zc = z - mu
    var = jnp.mean(zc * zc, axis=-1, keepdims=True)
    y = zc * lax.rsqrt(var + LN_EPS) * lg_ref[...] + lb_ref[...]
    y = y * (1.0 / (1.0 + jnp.exp(-y)))
    out_ref[...] = x_ref[...] + jnp.dot(y.astype(BF16), w_ref[...], preferred_element_type=F32) + b_ref[...]


def _ln_pw2(z, x, ln_g, ln_b, w, bias, tm):
    m = x.shape[0]
    est = 2 * (3 * tm * D_MODEL * 4 + D_MODEL * D_MODEL * 2) + 2 * tm * D_MODEL * 4
    row = lambda i: (i, 0)
    vec = pl.BlockSpec((1, D_MODEL), lambda i: (0, 0))
    return pl.pallas_call(
        _ln_pw2_kernel,
        grid=(m // tm,),
        in_specs=[pl.BlockSpec((tm, D_MODEL), row), pl.BlockSpec((tm, D_MODEL), row), vec, vec,
                  pl.BlockSpec((D_MODEL, D_MODEL), lambda i: (0, 0)), vec],
        out_specs=pl.BlockSpec((tm, D_MODEL), row),
        out_shape=jax.ShapeDtypeStruct((m, D_MODEL), F32),
        compiler_params=_params(("parallel",), est),
        name="ln_swish_pw2",
    )(z, x, ln_g.reshape(1, D_MODEL), ln_b.reshape(1, D_MODEL), w, bias.reshape(1, D_MODEL))


def kernel(x_prompt, x_sample, cache_kv_g0, cache_kv_g1, cache_kv_g2, state_conv, attn_norm, w_qkv, w_o,
           conv_norm, w_pw1, b_pw1, w_dw, b_dw, conv_ln_g, conv_ln_b, w_pw2, b_pw2, mlp_norm, w_up, w_down,
           final_norm):
    xp = x_prompt.reshape(MP, D_MODEL)
    xs = x_sample.reshape(MS, D_MODEL)

    wqkv = w_qkv[0].astype(BF16)
    wo = w_o[0].astype(BF16)
    q3p, kv01p, kv2p = _qkv(xp, attn_norm[0], wqkv, tm=1024)
    q3s, kv01s, kv2s = _qkv(xs, attn_norm[0], wqkv, tm=MS)

    o_p = _attn_prompt(q3p, kv01p, kv2p)
    xp = _wo(o_p, xp, wo, tm=512)

    o_s = _attn_sample(q3s, kv01s, kv2s, cache_kv_g0, cache_kv_g1, cache_kv_g2)
    xs = _wo(o_s, xs, wo, tm=MS)

    wu, wd = w_up.astype(BF16), w_down.astype(BF16)
    xp = _mlp(xp, mlp_norm[0], wu, wd, 0, final_norm, tm=1024, tf=1024, final=False)
    xs = _mlp(xs, mlp_norm[0], wu, wd, 0, final_norm, tm=MS, tf=1024, final=False)

    w1 = w_pw1[0].astype(BF16)
    w2 = w_pw2[0].astype(BF16)
    up = _pw1_glu(xp, conv_norm[0], w1, b_pw1[0], tm=1024, tn=512)
    us = _pw1_glu(xs, conv_norm[0], w1, b_pw1[0], tm=MS, tn=512)
    zp = _dw_prompt(up, w_dw[0], b_dw[0])
    zs, conv_sample = _dw_sample(state_conv, us, w_dw[0], b_dw[0])
    xp = _ln_pw2(zp, xp, conv_ln_g[0], conv_ln_b[0], w2, b_pw2[0], tm=512)
    xs = _ln_pw2(zs, xs, conv_ln_g[0], conv_ln_b[0], w2, b_pw2[0], tm=MS)

    yp = _mlp(xp, mlp_norm[1], wu, wd, 1, final_norm, tm=1024, tf=1024, final=True)
    ys = _mlp(xs, mlp_norm[1], wu, wd, 1, final_norm, tm=MS, tf=1024, final=True)

    def tail(kv, idx, keep):
        view = kv.reshape((-1, BATCH, SEQ, KV_W))
        return view[idx:idx + 1, :, SEQ - keep:].reshape(1, BATCH, keep, 2, N_HEADS, HEAD_DIM)

    def new_rows(kv):
        return kv.reshape(1, DEC_BATCH, DEC_SEQ, 2, N_HEADS, HEAD_DIM)

    conv_prompt = up.reshape(BATCH, SEQ, D_MODEL)[:, SEQ - HIST:][None]
    return (yp.reshape(BATCH, SEQ, D_MODEL), ys.reshape(DEC_BATCH, DEC_SEQ, D_MODEL),
            tail(kv01p, 0, WINDOWS[0]), tail(kv01p, 1, WINDOWS[1]), tail(kv2p, 0, WINDOWS[2]),
            conv_prompt,
            new_rows(kv01s[0]), new_rows(kv01s[1]), new_rows(kv2s),
            conv_sample)
```

```python
import functools

import numpy as np
import jax
import jax.numpy as jnp
from jax import lax
from jax.experimental import pallas as pl
from jax.experimental.pallas import tpu as pltpu

F32 = jnp.float32
BF16 = jnp.bfloat16

D_MODEL = 2048
BATCH = 8
SEQ = 2048
DEC_BATCH = 128
DEC_SEQ = 4
HEAD_DIM = 128
N_HEADS = 8
ATTN_WIDTH = N_HEADS * HEAD_DIM
KV_W = 2 * ATTN_WIDTH
WINDOWS = (128, 512, 2048)
DILATIONS = (1, 4, 16)
N_STEPS = 128
ATTN_SCALE = HEAD_DIM ** -0.5
CONV_WIDTH = 31
HIST = CONV_WIDTH - 1
D_FF = 4 * D_MODEL
RMS_EPS = 1e-6
LN_EPS = 1e-5
NEG = -1e30

MP = BATCH * SEQ
MS = DEC_BATCH * DEC_SEQ

V7X_VMEM_BYTES = 64 * 1024 * 1024
VMEM_HEADROOM_BYTES = 16 * 1024 * 1024


def _params(semantics, vmem_estimate):
    limit = min(int(vmem_estimate) + VMEM_HEADROOM_BYTES, V7X_VMEM_BYTES - 4 * 1024 * 1024)
    return pltpu.CompilerParams(dimension_semantics=semantics, vmem_limit_bytes=limit)


def _rms(x, g):
    y = x * lax.rsqrt(jnp.mean(x * x, axis=-1, keepdims=True) + RMS_EPS)
    return y * g


def _head_slopes():
    return (2.0 ** (-8.0 * np.arange(1, N_HEADS + 1) / N_HEADS)).astype(np.float32)


CAST_BLOCK_BYTES = 8 * 1024 * 1024


def _cast_kernel(w_ref, o_ref):
    o_ref[...] = w_ref[...].astype(BF16)


def _to_bf16(w):
    shape = w.shape
    w2 = w.reshape(-1, shape[-1])
    r, c = w2.shape
    rows = max(n for n in range(16, CAST_BLOCK_BYTES // (c * 4) + 1, 16) if r % n == 0)
    out = pl.pallas_call(
        _cast_kernel,
        grid=(r // rows,),
        in_specs=[pl.BlockSpec((rows, c), lambda i: (i, 0))],
        out_specs=pl.BlockSpec((rows, c), lambda i: (i, 0)),
        out_shape=jax.ShapeDtypeStruct((r, c), BF16),
        compiler_params=_params(("parallel",), 2 * rows * c * 6),
        name="cast_bf16",
    )(w2)
    return out.reshape(shape)


def _qkv_kernel(x_ref, g_ref, w_ref, q_ref, kv01_ref, kv2_ref, xn_ref):
    j = pl.program_id(1)

    @pl.when(j == 0)
    def _():
        xn_ref[...] = _rms(x_ref[...], g_ref[...]).astype(BF16)

    acc = jnp.dot(xn_ref[...], w_ref[...], preferred_element_type=F32)

    @pl.when(j < 3)
    def _():
        q_ref[0] = acc.astype(BF16)

    @pl.when(jnp.logical_and(j >= 3, j < 7))
    def _():
        kv01_ref[0] = acc

    @pl.when(j >= 7)
    def _():
        kv2_ref[...] = acc


def _qkv_wcol(j):
    jj = jnp.maximum(j - 3, 0)
    return jnp.where(j < 3, j, (1 + jj % 2) * 3 + jj // 2)


def _qkv(x, gain, w, tm):
    m = x.shape[0]
    tn = ATTN_WIDTH

    def kv01_idx(i, j):
        jj = jnp.clip(j - 3, 0, 3)
        return (jj // 2, i, jj % 2)

    est = 2 * (tm * D_MODEL * 4 + D_MODEL * tn * 2 + tm * tn * 2 + 2 * tm * tn * 4) + tm * D_MODEL * 2
    return pl.pallas_call(
        _qkv_kernel,
        grid=(m // tm, 9),
        in_specs=[
            pl.BlockSpec((tm, D_MODEL), lambda i, j: (i, 0)),
            pl.BlockSpec((1, D_MODEL), lambda i, j: (0, 0)),
            pl.BlockSpec((D_MODEL, tn), lambda i, j: (0, _qkv_wcol(j))),
        ],
        out_specs=[
            pl.BlockSpec((1, tm, tn), lambda i, j: (jnp.minimum(j, 2), i, 0)),
            pl.BlockSpec((1, tm, tn), kv01_idx),
            pl.BlockSpec((tm, tn), lambda i, j: (i, jnp.clip(j - 7, 0, 1))),
        ],
        out_shape=[
            jax.ShapeDtypeStruct((3, m, tn), BF16),
            jax.ShapeDtypeStruct((2, m, 2 * tn), F32),
            jax.ShapeDtypeStruct((m, 2 * tn), F32),
        ],
        scratch_shapes=[pltpu.VMEM((tm, D_MODEL), BF16)],
        compiler_params=_params(("parallel", "arbitrary"), est),
        name="rms_qkv",
    )(x, gain.reshape(1, D_MODEL), w)


QB = 128


def _attn_bias_table():
    steps = (QB + np.arange(QB)[:, None] - np.arange(2 * QB)[None, :]).astype(np.float32)
    valid = (steps >= 0) & (steps <= N_STEPS)
    bias = -_head_slopes()[:, None, None] * steps[None]
    return jnp.asarray(np.where(valid[None], bias, NEG).astype(np.float32))


def _attn_prompt_kernel(q0_ref, q1_ref, q2_ref, k0_ref, v0_ref, k1_ref, v1_ref, k2_ref, v2_ref, bias_ref, o_ref,
                        og0_ref, og1_ref, og2_ref, lg0_ref, lg1_ref, lg2_ref, qf_ref):
    nt = (((1,), (1,)), ((), ()))
    groups = ((q0_ref, k0_ref, v0_ref, og0_ref, lg0_ref), (q1_ref, k1_ref, v1_ref, og1_ref, lg1_ref),
              (q2_ref, k2_ref, v2_ref, og2_ref, lg2_ref))
    for g, (q_ref, k_ref, v_ref, og_ref, lg_ref) in enumerate(groups):
        dil = DILATIONS[g]
        if dil > 1:
            qf_ref[...] = q_ref[...].astype(F32)
            q_ref = qf_ref

        def rows(first_pos, n, r):
            start = r + dil * first_pos
            return pl.ds(start, n) if dil == 1 else pl.ds(start, n, stride=dil)

        for r in range(dil):
            for blk in range(SEQ // dil // QB):
                qrows = rows(blk * QB, QB, r)
                if blk == 0:
                    krows, bias = qrows, bias_ref[:, QB:]
                else:
                    krows, bias = rows((blk - 1) * QB, 2 * QB, r), bias_ref[...]
                s = lax.dot_general(q_ref[qrows, :].astype(BF16), k_ref[krows, :].astype(BF16), nt,
                                    preferred_element_type=F32)
                s = s * ATTN_SCALE + bias
                m = jnp.max(s, axis=-1, keepdims=True)
                p = jnp.exp(s - m)
                l = jnp.sum(p, axis=-1, keepdims=True)
                o = jnp.dot(p.astype(BF16), v_ref[krows, :].astype(BF16), preferred_element_type=F32)
                og_ref[qrows, :] = o * (1.0 / l)
                lg_ref[qrows, :] = jnp.broadcast_to(m + jnp.log(l), (QB, HEAD_DIM))

    for blk in range(SEQ // QB):
        rs = slice(blk * QB, (blk + 1) * QB)
        l0, l1, l2 = lg0_ref[rs, :], lg1_ref[rs, :], lg2_ref[rs, :]
        mx = jnp.maximum(jnp.maximum(l0, l1), l2)
        e0, e1, e2 = jnp.exp(l0 - mx), jnp.exp(l1 - mx), jnp.exp(l2 - mx)
        merged = (e0 * og0_ref[rs, :] + e1 * og1_ref[rs, :] + e2 * og2_ref[rs, :]) * (1.0 / (e0 + e1 + e2))
        o_ref[rs, :] = merged.astype(BF16)


def _attn_prompt(q3, kv01, kv2):
    bias = _attn_bias_table()
    qv = q3.reshape(3, BATCH, SEQ, ATTN_WIDTH)
    kv01v = kv01.reshape(2, BATCH, SEQ, KV_W)
    kv2v = kv2.reshape(BATCH, SEQ, KV_W)
    head = lambda g, off: pl.BlockSpec((None, None, SEQ, HEAD_DIM), lambda b, h: (g, b, 0, off + h))
    head2 = lambda off: pl.BlockSpec((None, SEQ, HEAD_DIM), lambda b, h: (b, 0, off + h))
    slab = SEQ * HEAD_DIM * 4
    est = 2 * (8 * slab + QB * 2 * QB * 4) + 7 * slab
    o = pl.pallas_call(
        _attn_prompt_kernel,
        grid=(BATCH, N_HEADS),
        in_specs=[head(0, 0), head(1, 0), head(2, 0),
                  head(0, 0), head(0, N_HEADS), head(1, 0), head(1, N_HEADS), head2(0), head2(N_HEADS),
                  pl.BlockSpec((None, QB, 2 * QB), lambda b, h: (h, 0, 0))],
        out_specs=pl.BlockSpec((None, SEQ, HEAD_DIM), lambda b, h: (b, 0, h)),
        out_shape=jax.ShapeDtypeStruct((BATCH, SEQ, ATTN_WIDTH), BF16),
        scratch_shapes=[pltpu.VMEM((SEQ, HEAD_DIM), F32)] * 7,
        compiler_params=_params(("parallel", "arbitrary"), est),
        name="attn_prompt",
    )(qv, qv, qv, kv01v, kv01v, kv01v, kv01v, kv2v, kv2v, bias)
    return o.reshape(MP, ATTN_WIDTH)


def _wo_kernel(o_ref, x_ref, w_ref, out_ref):
    out_ref[...] = x_ref[...] + jnp.dot(o_ref[...].astype(BF16), w_ref[...], preferred_element_type=F32)


def _wo(o, x, w, tm):
    m = x.shape[0]
    est = 2 * (tm * ATTN_WIDTH * 4 + 2 * tm * D_MODEL * 4 + ATTN_WIDTH * D_MODEL * 2)
    row = lambda i: (i, 0)
    return pl.pallas_call(
        _wo_kernel,
        grid=(m // tm,),
        in_specs=[pl.BlockSpec((tm, ATTN_WIDTH), row), pl.BlockSpec((tm, D_MODEL), row),
                  pl.BlockSpec((ATTN_WIDTH, D_MODEL), lambda i: (0, 0))],
        out_specs=pl.BlockSpec((tm, D_MODEL), row),
        out_shape=jax.ShapeDtypeStruct((m, D_MODEL), F32),
        compiler_params=_params(("parallel",), est),
        name="wo_residual",
    )(o, x, w)


S_COLS = DEC_SEQ * N_HEADS
SAMPLE_BB = 2
G0_TOKENS = WINDOWS[0]
GD_TOKENS = DEC_SEQ * N_STEPS


def _attn_sample_kernel(q_ref, kvn_ref, c0_ref, c1_ref, c2_ref, t0_ref, td_ref, tn_ref, e_ref, o_ref):
    nt = (((1,), (1,)), ((), ()))
    e_all = e_ref[DEC_SEQ]
    sub = lax.broadcasted_iota(jnp.int32, (N_HEADS, S_COLS), 0)
    col = lax.broadcasted_iota(jnp.int32, (N_HEADS, S_COLS), 1)
    tabs = (t0_ref, td_ref, td_ref)
    n_tok = (G0_TOKENS, GD_TOKENS, GD_TOKENS)

    def cached(g, bi, part):
        if g == 0:
            x = c0_ref[bi, :, part, :, :]
        elif g == 1:
            x = c1_ref[bi, :, part, :, :]
        else:
            x = c2_ref[bi, :, :, part, :, :].reshape(GD_TOKENS, N_HEADS, HEAD_DIM)
        return x.reshape(n_tok[g] * N_HEADS, HEAD_DIM)

    for bi in range(SAMPLE_BB):
        tok = slice(bi * DEC_SEQ, (bi + 1) * DEC_SEQ)
        s_buf, s_new = [], []
        m_all = None
        for g in range(3):
            q = q_ref[g, bi * S_COLS:(bi + 1) * S_COLS, :]
            s = lax.dot_general(cached(g, bi, 0).astype(BF16), q, nt, preferred_element_type=F32)
            s = (s * ATTN_SCALE + tabs[g][...]).reshape(n_tok[g], N_HEADS, S_COLS)
            kn = kvn_ref[g, tok, 0, :, :].reshape(S_COLS, HEAD_DIM)
            sn = lax.dot_general(kn.astype(BF16), q, nt, preferred_element_type=F32)
            sn = (sn * ATTN_SCALE + tn_ref[min(g, 1)]).reshape(DEC_SEQ, N_HEADS, S_COLS)
            s_buf.append(s)
            s_new.append(sn)
            cm = jnp.maximum(jnp.max(s, axis=0), jnp.max(sn, axis=0))
            m_all = cm if m_all is None else jnp.maximum(m_all, cm)

        m_all = jnp.where(col % N_HEADS == sub, m_all, 0.0)
        l_all = jnp.zeros((N_HEADS, S_COLS), F32)
        acc = [jnp.zeros((N_HEADS, HEAD_DIM), F32) for _ in range(DEC_SEQ)]
        for g in range(3):
            p = jnp.exp(s_buf[g] - m_all[None])
            pn = jnp.exp(s_new[g] - m_all[None])
            l_all = l_all + jnp.sum(p, axis=0) + jnp.sum(pn, axis=0)
            pb = p.reshape(n_tok[g] * N_HEADS, S_COLS).astype(BF16)
            pnb = pn.reshape(S_COLS, S_COLS).astype(BF16)
            v = cached(g, bi, 1)
            vn = kvn_ref[g, tok, 1, :, :].reshape(S_COLS, HEAD_DIM)
            if g == 0:
                for t in range(DEC_SEQ):
                    w = jnp.dot(pb, e_ref[t], preferred_element_type=F32) * v
                    wn = jnp.dot(pnb, e_ref[t], preferred_element_type=F32) * vn
                    acc[t] = (acc[t] + jnp.sum(w.reshape(G0_TOKENS, N_HEADS, HEAD_DIM), axis=0)
                              + jnp.sum(wn.reshape(DEC_SEQ, N_HEADS, HEAD_DIM), axis=0))
            else:
                w = jnp.dot(pb, e_all, preferred_element_type=F32) * v
                wn = jnp.dot(pnb, e_all, preferred_element_type=F32) * vn
                wq = jnp.sum(w.reshape(N_STEPS, DEC_SEQ, N_HEADS, HEAD_DIM), axis=0)
                wnq = wn.reshape(DEC_SEQ, N_HEADS, HEAD_DIM)
                for t in range(DEC_SEQ):
                    acc[t] = acc[t] + wq[t] + wnq[t]

        inv = 1.0 / l_all
        for t in range(DEC_SEQ):
            inv_t = jnp.sum(jnp.where(col == t * N_HEADS + sub, inv, 0.0), axis=-1, keepdims=True)
            o_ref[bi * DEC_SEQ + t] = acc[t] * inv_t


def _sample_tables():
    base = _head_slopes()
    cols = np.arange(S_COLS)
    ct, ch = (cols // N_HEADS)[None, :], (cols % N_HEADS)[None, :]
    slope = base[cols % N_HEADS][None, :]

    def rows(n_tokens):
        r = np.arange(n_tokens * N_HEADS)[:, None]
        return r // N_HEADS, r % N_HEADS

    m, h = rows(G0_TOKENS)
    t0 = np.where((h == ch) & (m >= ct), -slope * (G0_TOKENS + ct - m), NEG)
    j, h = rows(DEC_SEQ)
    tn0 = np.where((h == ch) & (j <= ct), -slope * (ct - j), NEG)
    tnd = np.where((h == ch) & (j == ct), 0.0, NEG)
    n, h = rows(GD_TOKENS)
    td = np.where((h == ch) & (n % DEC_SEQ == ct), -slope * (N_STEPS - n // DEC_SEQ), NEG)
    expand = np.zeros((DEC_SEQ + 1, S_COLS, HEAD_DIM), np.float32)
    for t in range(DEC_SEQ):
        expand[t, t * N_HEADS:(t + 1) * N_HEADS, :] = 1.0
    expand[DEC_SEQ] = 1.0
    f = lambda a: jnp.asarray(a.astype(np.float32))
    return f(t0), f(td), f(np.stack([tn0, tnd])), jnp.asarray(expand, dtype=BF16)


def _attn_sample(q3, kv01, kv2, cache0, cache1, cache2):
    t0, td, tn, expand = _sample_tables()
    kvn = jnp.concatenate([kv01, kv2[None]], axis=0).reshape(3, MS, 2, N_HEADS, HEAD_DIM)
    q = q3.astype(BF16).reshape(3, MS * N_HEADS, HEAD_DIM)
    c2v = cache2.reshape(1, DEC_BATCH, N_STEPS, DILATIONS[2], 2, N_HEADS, HEAD_DIM)
    bb = SAMPLE_BB
    est = 2 * bb * (G0_TOKENS + 2 * GD_TOKENS) * KV_W * 4 + 16 * 1024 * 1024
    full = lambda a: pl.BlockSpec(a.shape, lambda i: (0,) * a.ndim)
    o = pl.pallas_call(
        _attn_sample_kernel,
        grid=(DEC_BATCH // bb,),
        in_specs=[
            pl.BlockSpec((3, bb * S_COLS, HEAD_DIM), lambda i: (0, i, 0)),
            pl.BlockSpec((3, bb * DEC_SEQ, 2, N_HEADS, HEAD_DIM), lambda i: (0, i, 0, 0, 0)),
            pl.BlockSpec((None, bb, G0_TOKENS, 2, N_HEADS, HEAD_DIM), lambda i: (0, i, 0, 0, 0, 0)),
            pl.BlockSpec((None, bb, GD_TOKENS, 2, N_HEADS, HEAD_DIM), lambda i: (0, i, 0, 0, 0, 0)),
            pl.BlockSpec((None, bb, N_STEPS, DEC_SEQ, 2, N_HEADS, HEAD_DIM), lambda i: (0, i, 0, 0, 0, 0, 0)),
            full(t0), full(td), full(tn), full(expand),
        ],
        out_specs=pl.BlockSpec((bb * DEC_SEQ, N_HEADS, HEAD_DIM), lambda i: (i, 0, 0)),
        out_shape=jax.ShapeDtypeStruct((MS, N_HEADS, HEAD_DIM), F32),
        compiler_params=_params(("parallel",), est),
        name="attn_sample",
    )(q, kvn, cache0, cache1, c2v, t0, td, tn, expand)
    return o.reshape(MS, ATTN_WIDTH)


def _mlp_kernel(x_ref, g_ref, wu_ref, wd_ref, gf_ref, out_ref, xn_ref, *, final):
    f = pl.program_id(1)

    @pl.when(f == 0)
    def _():
        xn_ref[...] = _rms(x_ref[...], g_ref[...]).astype(BF16)
        out_ref[...] = jnp.zeros_like(out_ref)

    h = jnp.dot(xn_ref[...], wu_ref[...], preferred_element_type=F32)
    h = jnp.maximum(h, 0.0)
    h = (h * h).astype(BF16)
    out_ref[...] += jnp.dot(h, wd_ref[...], preferred_element_type=F32)

    @pl.when(f == pl.num_programs(1) - 1)
    def _():
        y = x_ref[...] + out_ref[...]
        if final:
            y = _rms(y, gf_ref[...])
        out_ref[...] = y


def _mlp(x, gain, w_up, w_down, layer, final_gain, tm, tf, final):
    m = x.shape[0]
    est = 4 * tm * D_MODEL * 4 + 2 * (2 * D_MODEL * tf * 2) + tm * D_MODEL * 2 + tm * tf * 6
    return pl.pallas_call(
        functools.partial(_mlp_kernel, final=final),
        grid=(m // tm, D_FF // tf),
        in_specs=[
            pl.BlockSpec((tm, D_MODEL), lambda i, f: (i, 0)),
            pl.BlockSpec((1, D_MODEL), lambda i, f: (0, 0)),
            pl.BlockSpec((None, D_MODEL, tf), lambda i, f: (layer, 0, f)),
            pl.BlockSpec((None, tf, D_MODEL), lambda i, f: (layer, f, 0)),
            pl.BlockSpec((1, D_MODEL), lambda i, f: (0, 0)),
        ],
        out_specs=pl.BlockSpec((tm, D_MODEL), lambda i, f: (i, 0)),
        out_shape=jax.ShapeDtypeStruct((m, D_MODEL), F32),
        scratch_shapes=[pltpu.VMEM((tm, D_MODEL), BF16)],
        compiler_params=_params(("parallel", "arbitrary"), est),
        name="mlp_final" if final else "mlp",
    )(x, gain.reshape(1, D_MODEL), w_up, w_down, final_gain.reshape(1, D_MODEL))


def _pw1_kernel(x_ref, g_ref, wa_ref, wb_ref, ba_ref, bb_ref, u_ref, xn_ref):
    @pl.when(pl.program_id(1) == 0)
    def _():
        xn_ref[...] = _rms(x_ref[...], g_ref[...]).astype(BF16)

    xn = xn_ref[...]
    a = jnp.dot(xn, wa_ref[...], preferred_element_type=F32) + ba_ref[...]
    b = jnp.dot(xn, wb_ref[...], preferred_element_type=F32) + bb_ref[...]
    u_ref[...] = a * (1.0 / (1.0 + jnp.exp(-b)))


def _pw1_glu(x, gain, w, bias, tm, tn):
    m = x.shape[0]
    nj = D_MODEL // tn
    b2 = bias.reshape(1, 2 * D_MODEL)
    est = 2 * (tm * D_MODEL * 4 + 2 * D_MODEL * tn * 2 + tm * tn * 4) + tm * D_MODEL * 2 + 2 * tm * tn * 4
    return pl.pallas_call(
        _pw1_kernel,
        grid=(m // tm, nj),
        in_specs=[
            pl.BlockSpec((tm, D_MODEL), lambda i, j: (i, 0)),
            pl.BlockSpec((1, D_MODEL), lambda i, j: (0, 0)),
            pl.BlockSpec((D_MODEL, tn), lambda i, j: (0, j)),
            pl.BlockSpec((D_MODEL, tn), lambda i, j: (0, nj + j)),
            pl.BlockSpec((1, tn), lambda i, j: (0, j)),
            pl.BlockSpec((1, tn), lambda i, j: (0, nj + j)),
        ],
        out_specs=pl.BlockSpec((tm, tn), lambda i, j: (i, j)),
        out_shape=jax.ShapeDtypeStruct((m, D_MODEL), F32),
        scratch_shapes=[pltpu.VMEM((tm, D_MODEL), BF16)],
        compiler_params=_params(("parallel", "arbitrary"), est),
        name="pw1_glu",
    )(x, gain.reshape(1, D_MODEL), w, w, b2, b2)


CONV_TS = 256
CONV_HALO = 32


def _dw_prompt_kernel(prev_ref, u_ref, w_ref, b_ref, z_ref, ext_ref, sh_ref):
    i = pl.program_id(1)

    @pl.when(i == 0)
    def _():
        ext_ref[0:CONV_HALO, :] = jnp.zeros((CONV_HALO, D_MODEL), F32)

    @pl.when(i > 0)
    def _():
        ext_ref[0:CONV_HALO, :] = prev_ref[...]

    ext_ref[CONV_HALO:, :] = u_ref[...]
    off = CONV_HALO - HIST
    span = CONV_TS + CONV_HALO - 8

    def chunk(c, carry):
        cs = pl.ds(pl.multiple_of(c * 128, 128), 128)
        for phase in range(1, 8):
            sh_ref[phase - 1] = ext_ref[pl.ds(phase, span), cs]
        acc = jnp.zeros((CONV_TS, 128), F32) + b_ref[:, cs]
        for j in range(CONV_WIDTH):
            a, phase = divmod(j + off, 8)
            if phase == 0:
                src = ext_ref[pl.ds(8 * a, CONV_TS), cs]
            else:
                src = sh_ref[phase - 1, pl.ds(8 * a, CONV_TS), :]
            acc = acc + src * w_ref[pl.ds(j, 1), cs]
        z_ref[:, cs] = acc
        return carry

    lax.fori_loop(0, D_MODEL // 128, chunk, 0)


def _dw_prompt(u, w_dw, b_dw):
    u3 = u.reshape(BATCH, SEQ, D_MODEL)
    nt = SEQ // CONV_TS
    per = CONV_TS // CONV_HALO
    est = 2 * (CONV_HALO + 2 * CONV_TS) * D_MODEL * 4 + (CONV_TS + CONV_HALO) * D_MODEL * 4 + 2 * 32 * D_MODEL * 4
    z = pl.pallas_call(
        _dw_prompt_kernel,
        grid=(BATCH, nt),
        in_specs=[
            pl.BlockSpec((None, CONV_HALO, D_MODEL), lambda b, i: (b, jnp.maximum(i * per - 1, 0), 0)),
            pl.BlockSpec((None, CONV_TS, D_MODEL), lambda b, i: (b, i, 0)),
            pl.BlockSpec((CONV_WIDTH, D_MODEL), lambda b, i: (0, 0)),
            pl.BlockSpec((1, D_MODEL), lambda b, i: (0, 0)),
        ],
        out_specs=pl.BlockSpec((None, CONV_TS, D_MODEL), lambda b, i: (b, i, 0)),
        out_shape=jax.ShapeDtypeStruct((BATCH, SEQ, D_MODEL), F32),
        scratch_shapes=[pltpu.VMEM((CONV_TS + CONV_HALO, D_MODEL), F32),
                        pltpu.VMEM((7, CONV_TS + CONV_HALO - 8, 128), F32)],
        compiler_params=_params(("parallel", "arbitrary"), est),
        name="dwconv_prompt",
    )(u3, u3, w_dw, b_dw.reshape(1, D_MODEL))
    return z.reshape(MP, D_MODEL)


DW_BB = 8


def _dw_sample_kernel(hist_ref, u_ref, wh_ref, wu_ref, b_ref, z_ref, new_ref):
    for b in range(DW_BB):
        hist = hist_ref[b]
        u = u_ref[b]
        for t in range(DEC_SEQ):
            z = jnp.sum(wh_ref[t] * hist, axis=0, keepdims=True)
            z = z + jnp.sum(wu_ref[t] * u, axis=0, keepdims=True)
            z_ref[b, t:t + 1, :] = z + b_ref[...]
        new_ref[b, 0:HIST - DEC_SEQ, :] = hist[DEC_SEQ:, :]
        new_ref[b, HIST - DEC_SEQ:, :] = u


def _dw_sample(state, u, w_dw, b_dw):
    zeros = jnp.zeros((DEC_SEQ, D_MODEL), F32)
    wh = jnp.stack([jnp.concatenate([zeros[:t], w_dw[:HIST - t]], axis=0) for t in range(DEC_SEQ)])
    wu = jnp.stack([jnp.concatenate([w_dw[HIST - t:], zeros[:DEC_SEQ - 1 - t]], axis=0) for t in range(DEC_SEQ)])
    u3 = u.reshape(DEC_BATCH, DEC_SEQ, D_MODEL)
    est = 4 * (DW_BB * (32 + 32 + 8 + 8) * D_MODEL * 4) + 2 * DEC_SEQ * 40 * D_MODEL * 4
    z, new_state = pl.pallas_call(
        _dw_sample_kernel,
        grid=(DEC_BATCH // DW_BB,),
        in_specs=[
            pl.BlockSpec((None, DW_BB, HIST, D_MODEL), lambda i: (0, i, 0, 0)),
            pl.BlockSpec((DW_BB, DEC_SEQ, D_MODEL), lambda i: (i, 0, 0)),
            pl.BlockSpec((DEC_SEQ, HIST, D_MODEL), lambda i: (0, 0, 0)),
            pl.BlockSpec((DEC_SEQ, DEC_SEQ, D_MODEL), lambda i: (0, 0, 0)),
            pl.BlockSpec((1, D_MODEL), lambda i: (0, 0)),
        ],
        out_specs=[pl.BlockSpec((DW_BB, DEC_SEQ, D_MODEL), lambda i: (i, 0, 0)),
                   pl.BlockSpec((None, DW_BB, HIST, D_MODEL), lambda i: (0, i, 0, 0))],
        out_shape=[jax.ShapeDtypeStruct((DEC_BATCH, DEC_SEQ, D_MODEL), F32),
                   jax.ShapeDtypeStruct((1, DEC_BATCH, HIST, D_MODEL), F32)],
        compiler_params=_params(("parallel",), est),
        name="dwconv_sample",
    )(state, u3, wh, wu, b_dw.reshape(1, D_MODEL))
    return z.reshape(MS, D_MODEL), new_state


def _ln_pw2_kernel(z_ref, x_ref, lg_ref, lb_ref, w_ref, b_ref, out_ref):
    z = z_ref[...]
    mu = jnp.mean(z, axis=-1, keepdims=True)
    zc = z - mu
    var = jnp.mean(zc * zc, axis=-1, keepdims=True)
    y = zc * lax.rsqrt(var + LN_EPS) * lg_ref[...] + lb_ref[...]
    y = y * (1.0 / (1.0 + jnp.exp(-y)))
    out_ref[...] = x_ref[...] + jnp.dot(y.astype(BF16), w_ref[...], preferred_element_type=F32) + b_ref[...]


def _ln_pw2(z, x, ln_g, ln_b, w, bias, tm):
    m = x.shape[0]
    est = 2 * (3 * tm * D_MODEL * 4 + D_MODEL * D_MODEL * 2) + 2 * tm * D_MODEL * 4
    row = lambda i: (i, 0)
    vec = pl.BlockSpec((1, D_MODEL), lambda i: (0, 0))
    return pl.pallas_call(
        _ln_pw2_kernel,
        grid=(m // tm,),
        in_specs=[pl.BlockSpec((tm, D_MODEL), row), pl.BlockSpec((tm, D_MODEL), row), vec, vec,
                  pl.BlockSpec((D_MODEL, D_MODEL), lambda i: (0, 0)), vec],
        out_specs=pl.BlockSpec((tm, D_MODEL), row),
        out_shape=jax.ShapeDtypeStruct((m, D_MODEL), F32),
        compiler_params=_params(("parallel",), est),
        name="ln_swish_pw2",
    )(z, x, ln_g.reshape(1, D_MODEL), ln_b.reshape(1, D_MODEL), w, bias.reshape(1, D_MODEL))


def kernel(x_prompt, x_sample, cache_kv_g0, cache_kv_g1, cache_kv_g2, state_conv, attn_norm, w_qkv, w_o,
           conv_norm, w_pw1, b_pw1, w_dw, b_dw, conv_ln_g, conv_ln_b, w_pw2, b_pw2, mlp_norm, w_up, w_down,
           final_norm):
    xp = x_prompt.reshape(MP, D_MODEL)
    xs = x_sample.reshape(MS, D_MODEL)

    wqkv = _to_bf16(w_qkv)[0]
    wo = w_o[0].astype(BF16)
    q3p, kv01p, kv2p = _qkv(xp, attn_norm[0], wqkv, tm=1024)
    q3s, kv01s, kv2s = _qkv(xs, attn_norm[0], wqkv, tm=MS)

    o_p = _attn_prompt(q3p, kv01p, kv2p)
    xp = _wo(o_p, xp, wo, tm=512)

    o_s = _attn_sample(q3s, kv01s, kv2s, cache_kv_g0, cache_kv_g1, cache_kv_g2)
    xs = _wo(o_s, xs, wo, tm=MS)

    wu, wd = _to_bf16(w_up), _to_bf16(w_down)
    xp = _mlp(xp, mlp_norm[0], wu, wd, 0, final_norm, tm=1024, tf=512, final=False)
    xs = _mlp(xs, mlp_norm[0], wu, wd, 0, final_norm, tm=MS, tf=1024, final=False)

    w1 = w_pw1[0].astype(BF16)
    w2 = w_pw2[0].astype(BF16)
    up = _pw1_glu(xp, conv_norm[0], w1, b_pw1[0], tm=1024, tn=512)
    us = _pw1_glu(xs, conv_norm[0], w1, b_pw1[0], tm=MS, tn=512)
    zp = _dw_prompt(up, w_dw[0], b_dw[0])
    zs, conv_sample = _dw_sample(state_conv, us, w_dw[0], b_dw[0])
    xp = _ln_pw2(zp, xp, conv_ln_g[0], conv_ln_b[0], w2, b_pw2[0], tm=512)
    xs = _ln_pw2(zs, xs, conv_ln_g[0], conv_ln_b[0], w2, b_pw2[0], tm=MS)

    yp = _mlp(xp, mlp_norm[1], wu, wd, 1, final_norm, tm=1024, tf=512, final=True)
    ys = _mlp(xs, mlp_norm[1], wu, wd, 1, final_norm, tm=MS, tf=1024, final=True)

    def tail(kv, idx, keep):
        view = kv.reshape((-1, BATCH, SEQ, KV_W))
        return view[idx:idx + 1, :, SEQ - keep:].reshape(1, BATCH, keep, 2, N_HEADS, HEAD_DIM)

    def new_rows(kv):
        return kv.reshape(1, DEC_BATCH, DEC_SEQ, 2, N_HEADS, HEAD_DIM)

    conv_prompt = up.reshape(BATCH, SEQ, D_MODEL)[:, SEQ - HIST:][None]
    return (yp.reshape(BATCH, SEQ, D_MODEL), ys.reshape(DEC_BATCH, DEC_SEQ, D_MODEL),
            tail(kv01p, 0, WINDOWS[0]), tail(kv01p, 1, WINDOWS[1]), tail(kv2p, 0, WINDOWS[2]),
            conv_prompt,
            new_rows(kv01s[0]), new_rows(kv01s[1]), new_rows(kv2s),
            conv_sample)
```

```python
import functools

import numpy as np
import jax
import jax.numpy as jnp
from jax import lax
from jax.experimental import pallas as pl
from jax.experimental.pallas import tpu as pltpu

F32 = jnp.float32
BF16 = jnp.bfloat16

D_MODEL = 2048
BATCH = 8
SEQ = 2048
DEC_BATCH = 128
DEC_SEQ = 4
HEAD_DIM = 128
N_HEADS = 8
ATTN_WIDTH = N_HEADS * HEAD_DIM
KV_W = 2 * ATTN_WIDTH
WINDOWS = (128, 512, 2048)
DILATIONS = (1, 4, 16)
N_STEPS = 128
ATTN_SCALE = HEAD_DIM ** -0.5
CONV_WIDTH = 31
HIST = CONV_WIDTH - 1
D_FF = 4 * D_MODEL
RMS_EPS = 1e-6
LN_EPS = 1e-5
NEG = -1e30

MP = BATCH * SEQ
MS = DEC_BATCH * DEC_SEQ

V7X_VMEM_BYTES = 64 * 1024 * 1024
VMEM_HEADROOM_BYTES = 16 * 1024 * 1024


def _params(semantics, vmem_estimate):
    limit = min(int(vmem_estimate) + VMEM_HEADROOM_BYTES, V7X_VMEM_BYTES - 4 * 1024 * 1024)
    return pltpu.CompilerParams(dimension_semantics=semantics, vmem_limit_bytes=limit)


def _rms(x, g):
    y = x * lax.rsqrt(jnp.mean(x * x, axis=-1, keepdims=True) + RMS_EPS)
    return y * g


def _head_slopes():
    return (2.0 ** (-8.0 * np.arange(1, N_HEADS + 1) / N_HEADS)).astype(np.float32)


CAST_BLOCK_BYTES = 8 * 1024 * 1024


def _cast_kernel(w_ref, o_ref):
    o_ref[...] = w_ref[...].astype(BF16)


def _to_bf16(w):
    shape = w.shape
    w2 = w.reshape(-1, shape[-1])
    r, c = w2.shape
    rows = max(n for n in range(16, CAST_BLOCK_BYTES // (c * 4) + 1, 16) if r % n == 0)
    out = pl.pallas_call(
        _cast_kernel,
        grid=(r // rows,),
        in_specs=[pl.BlockSpec((rows, c), lambda i: (i, 0))],
        out_specs=pl.BlockSpec((rows, c), lambda i: (i, 0)),
        out_shape=jax.ShapeDtypeStruct((r, c), BF16),
        compiler_params=_params(("parallel",), 2 * rows * c * 6),
        name="cast_bf16",
    )(w2)
    return out.reshape(shape)


def _qkv_kernel(x_ref, g_ref, w_ref, q_ref, kv01_ref, kv2_ref, xn_ref):
    j = pl.program_id(1)

    @pl.when(j == 0)
    def _():
        xn_ref[...] = _rms(x_ref[...], g_ref[...]).astype(BF16)

    acc = jnp.dot(xn_ref[...], w_ref[...], preferred_element_type=F32)

    @pl.when(j < 3)
    def _():
        q_ref[0] = acc.astype(BF16)

    @pl.when(jnp.logical_and(j >= 3, j < 7))
    def _():
        kv01_ref[0] = acc

    @pl.when(j >= 7)
    def _():
        kv2_ref[...] = acc


def _qkv_wcol(j):
    jj = jnp.maximum(j - 3, 0)
    return jnp.where(j < 3, j, (1 + jj % 2) * 3 + jj // 2)


def _qkv(x, gain, w, tm):
    m = x.shape[0]
    tn = ATTN_WIDTH

    def kv01_idx(i, j):
        jj = jnp.clip(j - 3, 0, 3)
        return (jj // 2, i, jj % 2)

    est = 2 * (tm * D_MODEL * 4 + D_MODEL * tn * 2 + tm * tn * 2 + 2 * tm * tn * 4) + tm * D_MODEL * 2
    return pl.pallas_call(
        _qkv_kernel,
        grid=(m // tm, 9),
        in_specs=[
            pl.BlockSpec((tm, D_MODEL), lambda i, j: (i, 0)),
            pl.BlockSpec((1, D_MODEL), lambda i, j: (0, 0)),
            pl.BlockSpec((D_MODEL, tn), lambda i, j: (0, _qkv_wcol(j))),
        ],
        out_specs=[
            pl.BlockSpec((1, tm, tn), lambda i, j: (jnp.minimum(j, 2), i, 0)),
            pl.BlockSpec((1, tm, tn), kv01_idx),
            pl.BlockSpec((tm, tn), lambda i, j: (i, jnp.clip(j - 7, 0, 1))),
        ],
        out_shape=[
            jax.ShapeDtypeStruct((3, m, tn), BF16),
            jax.ShapeDtypeStruct((2, m, 2 * tn), F32),
            jax.ShapeDtypeStruct((m, 2 * tn), F32),
        ],
        scratch_shapes=[pltpu.VMEM((tm, D_MODEL), BF16)],
        compiler_params=_params(("parallel", "arbitrary"), est),
        name="rms_qkv",
    )(x, gain.reshape(1, D_MODEL), w)


QB = 128
ATTN_WAY = 4


def _attn_bias_table():
    steps = (QB + np.arange(QB)[:, None] - np.arange(2 * QB)[None, :]).astype(np.float32)
    valid = (steps >= 0) & (steps <= N_STEPS)
    bias = -_head_slopes()[:, None, None] * steps[None]
    return jnp.asarray(np.where(valid[None], bias, NEG).astype(np.float32))


def _attn_prompt_kernel(q0_ref, q1_ref, q2_ref, k0_ref, v0_ref, k1_ref, v1_ref, k2_ref, v2_ref, bias_ref, o_ref,
                        og0_ref, og1_ref, og2_ref, lg0_ref, lg1_ref, lg2_ref, qf_ref):
    nt = (((1,), (1,)), ((), ()))
    groups = ((q0_ref, k0_ref, v0_ref, og0_ref, lg0_ref), (q1_ref, k1_ref, v1_ref, og1_ref, lg1_ref),
              (q2_ref, k2_ref, v2_ref, og2_ref, lg2_ref))
    for g, (q_ref, k_ref, v_ref, og_ref, lg_ref) in enumerate(groups):
        dil = DILATIONS[g]
        if dil > 1:
            qf_ref[...] = q_ref[...].astype(F32)
            q_ref = qf_ref

        def rows(first_pos, n, r):
            start = r + dil * first_pos
            return pl.ds(start, n) if dil == 1 else pl.ds(start, n, stride=dil)

        blocks = [(r, blk) for r in range(dil) for blk in range(SEQ // dil // QB)]
        for b0 in range(0, len(blocks), ATTN_WAY):
            scored = []
            for r, blk in blocks[b0:b0 + ATTN_WAY]:
                qrows = rows(blk * QB, QB, r)
                if blk == 0:
                    krows, bias = qrows, bias_ref[:, QB:]
                else:
                    krows, bias = rows((blk - 1) * QB, 2 * QB, r), bias_ref[...]
                s = lax.dot_general(q_ref[qrows, :].astype(BF16), k_ref[krows, :].astype(BF16), nt,
                                    preferred_element_type=F32)
                scored.append((qrows, krows, s * ATTN_SCALE + bias))
            weighted = []
            for qrows, krows, s in scored:
                m = jnp.max(s, axis=-1, keepdims=True)
                p = jnp.exp(s - m)
                l = jnp.sum(p, axis=-1, keepdims=True)
                weighted.append((qrows, krows, p, m, l))
            for qrows, krows, p, m, l in weighted:
                o = jnp.dot(p.astype(BF16), v_ref[krows, :].astype(BF16), preferred_element_type=F32)
                og_ref[qrows, :] = o * (1.0 / l)
                lg_ref[qrows, :] = jnp.broadcast_to(m + jnp.log(l), (QB, HEAD_DIM))

    for blk in range(SEQ // QB):
        rs = slice(blk * QB, (blk + 1) * QB)
        l0, l1, l2 = lg0_ref[rs, :], lg1_ref[rs, :], lg2_ref[rs, :]
        mx = jnp.maximum(jnp.maximum(l0, l1), l2)
        e0, e1, e2 = jnp.exp(l0 - mx), jnp.exp(l1 - mx), jnp.exp(l2 - mx)
        merged = (e0 * og0_ref[rs, :] + e1 * og1_ref[rs, :] + e2 * og2_ref[rs, :]) * (1.0 / (e0 + e1 + e2))
        o_ref[rs, :] = merged.astype(BF16)


def _attn_prompt(q3, kv01, kv2):
    bias = _attn_bias_table()
    qv = q3.reshape(3, BATCH, SEQ, ATTN_WIDTH)
    kv01v = kv01.reshape(2, BATCH, SEQ, KV_W)
    kv2v = kv2.reshape(BATCH, SEQ, KV_W)
    head = lambda g, off: pl.BlockSpec((None, None, SEQ, HEAD_DIM), lambda b, h: (g, b, 0, off + h))
    head2 = lambda off: pl.BlockSpec((None, SEQ, HEAD_DIM), lambda b, h: (b, 0, off + h))
    slab = SEQ * HEAD_DIM * 4
    est = 2 * (8 * slab + QB * 2 * QB * 4) + 7 * slab
    o = pl.pallas_call(
        _attn_prompt_kernel,
        grid=(BATCH, N_HEADS),
        in_specs=[head(0, 0), head(1, 0), head(2, 0),
                  head(0, 0), head(0, N_HEADS), head(1, 0), head(1, N_HEADS), head2(0), head2(N_HEADS),
                  pl.BlockSpec((None, QB, 2 * QB), lambda b, h: (h, 0, 0))],
        out_specs=pl.BlockSpec((None, SEQ, HEAD_DIM), lambda b, h: (b, 0, h)),
        out_shape=jax.ShapeDtypeStruct((BATCH, SEQ, ATTN_WIDTH), BF16),
        scratch_shapes=[pltpu.VMEM((SEQ, HEAD_DIM), F32)] * 7,
        compiler_params=_params(("parallel", "arbitrary"), est),
        name="attn_prompt",
    )(qv, qv, qv, kv01v, kv01v, kv01v, kv01v, kv2v, kv2v, bias)
    return o.reshape(MP, ATTN_WIDTH)


def _wo_kernel(o_ref, x_ref, w_ref, out_ref):
    out_ref[...] = x_ref[...] + jnp.dot(o_ref[...].astype(BF16), w_ref[...], preferred_element_type=F32)


def _wo(o, x, w, tm):
    m = x.shape[0]
    est = 2 * (tm * ATTN_WIDTH * 4 + 2 * tm * D_MODEL * 4 + ATTN_WIDTH * D_MODEL * 2)
    row = lambda i: (i, 0)
    return pl.pallas_call(
        _wo_kernel,
        grid=(m // tm,),
        in_specs=[pl.BlockSpec((tm, ATTN_WIDTH), row), pl.BlockSpec((tm, D_MODEL), row),
                  pl.BlockSpec((ATTN_WIDTH, D_MODEL), lambda i: (0, 0))],
        out_specs=pl.BlockSpec((tm, D_MODEL), row),
        out_shape=jax.ShapeDtypeStruct((m, D_MODEL), F32),
        compiler_params=_params(("parallel",), est),
        name="wo_residual",
    )(o, x, w)


S_COLS = DEC_SEQ * N_HEADS
SAMPLE_BB = 2
G0_TOKENS = WINDOWS[0]
GD_TOKENS = DEC_SEQ * N_STEPS


def _attn_sample_kernel(q_ref, kvn_ref, c0_ref, c1_ref, c2_ref, t0_ref, td_ref, tn_ref, e_ref, o_ref):
    nt = (((1,), (1,)), ((), ()))
    e_all = e_ref[DEC_SEQ]
    sub = lax.broadcasted_iota(jnp.int32, (N_HEADS, S_COLS), 0)
    col = lax.broadcasted_iota(jnp.int32, (N_HEADS, S_COLS), 1)
    tabs = (t0_ref, td_ref, td_ref)
    n_tok = (G0_TOKENS, GD_TOKENS, GD_TOKENS)

    def cached(g, bi, part):
        if g == 0:
            x = c0_ref[bi, :, part, :, :]
        elif g == 1:
            x = c1_ref[bi, :, part, :, :]
        else:
            x = c2_ref[bi, :, :, part, :, :].reshape(GD_TOKENS, N_HEADS, HEAD_DIM)
        return x.reshape(n_tok[g] * N_HEADS, HEAD_DIM)

    for bi in range(SAMPLE_BB):
        tok = slice(bi * DEC_SEQ, (bi + 1) * DEC_SEQ)
        s_buf, s_new = [], []
        m_all = None
        for g in range(3):
            q = q_ref[g, bi * S_COLS:(bi + 1) * S_COLS, :]
            s = lax.dot_general(cached(g, bi, 0).astype(BF16), q, nt, preferred_element_type=F32)
            s = (s * ATTN_SCALE + tabs[g][...]).reshape(n_tok[g], N_HEADS, S_COLS)
            kn = kvn_ref[g, tok, 0, :, :].reshape(S_COLS, HEAD_DIM)
            sn = lax.dot_general(kn.astype(BF16), q, nt, preferred_element_type=F32)
            sn = (sn * ATTN_SCALE + tn_ref[min(g, 1)]).reshape(DEC_SEQ, N_HEADS, S_COLS)
            s_buf.append(s)
            s_new.append(sn)
            cm = jnp.maximum(jnp.max(s, axis=0), jnp.max(sn, axis=0))
            m_all = cm if m_all is None else jnp.maximum(m_all, cm)

        m_all = jnp.where(col % N_HEADS == sub, m_all, 0.0)
        l_all = jnp.zeros((N_HEADS, S_COLS), F32)
        acc = [jnp.zeros((N_HEADS, HEAD_DIM), F32) for _ in range(DEC_SEQ)]
        for g in range(3):
            p = jnp.exp(s_buf[g] - m_all[None])
            pn = jnp.exp(s_new[g] - m_all[None])
            l_all = l_all + jnp.sum(p, axis=0) + jnp.sum(pn, axis=0)
            pb = p.reshape(n_tok[g] * N_HEADS, S_COLS).astype(BF16)
            pnb = pn.reshape(S_COLS, S_COLS).astype(BF16)
            v = cached(g, bi, 1)
            vn = kvn_ref[g, tok, 1, :, :].reshape(S_COLS, HEAD_DIM)
            if g == 0:
                for t in range(DEC_SEQ):
                    w = jnp.dot(pb, e_ref[t], preferred_element_type=F32) * v
                    wn = jnp.dot(pnb, e_ref[t], preferred_element_type=F32) * vn
                    acc[t] = (acc[t] + jnp.sum(w.reshape(G0_TOKENS, N_HEADS, HEAD_DIM), axis=0)
                              + jnp.sum(wn.reshape(DEC_SEQ, N_HEADS, HEAD_DIM), axis=0))
            else:
                w = jnp.dot(pb, e_all, preferred_element_type=F32) * v
                wn = jnp.dot(pnb, e_all, preferred_element_type=F32) * vn
                wq = jnp.sum(w.reshape(N_STEPS, DEC_SEQ, N_HEADS, HEAD_DIM), axis=0)
                wnq = wn.reshape(DEC_SEQ, N_HEADS, HEAD_DIM)
                for t in range(DEC_SEQ):
                    acc[t] = acc[t] + wq[t] + wnq[t]

        inv = 1.0 / l_all
        for t in range(DEC_SEQ):
            inv_t = jnp.sum(jnp.where(col == t * N_HEADS + sub, inv, 0.0), axis=-1, keepdims=True)
            o_ref[bi * DEC_SEQ + t] = acc[t] * inv_t


def _sample_tables():
    base = _head_slopes()
    cols = np.arange(S_COLS)
    ct, ch = (cols // N_HEADS)[None, :], (cols % N_HEADS)[None, :]
    slope = base[cols % N_HEADS][None, :]

    def rows(n_tokens):
        r = np.arange(n_tokens * N_HEADS)[:, None]
        return r // N_HEADS, r % N_HEADS

    m, h = rows(G0_TOKENS)
    t0 = np.where((h == ch) & (m >= ct), -slope * (G0_TOKENS + ct - m), NEG)
    j, h = rows(DEC_SEQ)
    tn0 = np.where((h == ch) & (j <= ct), -slope * (ct - j), NEG)
    tnd = np.where((h == ch) & (j == ct), 0.0, NEG)
    n, h = rows(GD_TOKENS)
    td = np.where((h == ch) & (n % DEC_SEQ == ct), -slope * (N_STEPS - n // DEC_SEQ), NEG)
    expand = np.zeros((DEC_SEQ + 1, S_COLS, HEAD_DIM), np.float32)
    for t in range(DEC_SEQ):
        expand[t, t * N_HEADS:(t + 1) * N_HEADS, :] = 1.0
    expand[DEC_SEQ] = 1.0
    f = lambda a: jnp.asarray(a.astype(np.float32))
    return f(t0), f(td), f(np.stack([tn0, tnd])), jnp.asarray(expand, dtype=BF16)


def _attn_sample(q3, kv01, kv2, cache0, cache1, cache2):
    t0, td, tn, expand = _sample_tables()
    kvn = jnp.concatenate([kv01, kv2[None]], axis=0).reshape(3, MS, 2, N_HEADS, HEAD_DIM)
    q = q3.astype(BF16).reshape(3, MS * N_HEADS, HEAD_DIM)
    c2v = cache2.reshape(1, DEC_BATCH, N_STEPS, DILATIONS[2], 2, N_HEADS, HEAD_DIM)
    bb = SAMPLE_BB
    est = 2 * bb * (G0_TOKENS + 2 * GD_TOKENS) * KV_W * 4 + 16 * 1024 * 1024
    full = lambda a: pl.BlockSpec(a.shape, lambda i: (0,) * a.ndim)
    o = pl.pallas_call(
        _attn_sample_kernel,
        grid=(DEC_BATCH // bb,),
        in_specs=[
            pl.BlockSpec((3, bb * S_COLS, HEAD_DIM), lambda i: (0, i, 0)),
            pl.BlockSpec((3, bb * DEC_SEQ, 2, N_HEADS, HEAD_DIM), lambda i: (0, i, 0, 0, 0)),
            pl.BlockSpec((None, bb, G0_TOKENS, 2, N_HEADS, HEAD_DIM), lambda i: (0, i, 0, 0, 0, 0)),
            pl.BlockSpec((None, bb, GD_TOKENS, 2, N_HEADS, HEAD_DIM), lambda i: (0, i, 0, 0, 0, 0)),
            pl.BlockSpec((None, bb, N_STEPS, DEC_SEQ, 2, N_HEADS, HEAD_DIM), lambda i: (0, i, 0, 0, 0, 0, 0)),
            full(t0), full(td), full(tn), full(expand),
        ],
        out_specs=pl.BlockSpec((bb * DEC_SEQ, N_HEADS, HEAD_DIM), lambda i: (i, 0, 0)),
        out_shape=jax.ShapeDtypeStruct((MS, N_HEADS, HEAD_DIM), F32),
        compiler_params=_params(("parallel",), est),
        name="attn_sample",
    )(q, kvn, cache0, cache1, c2v, t0, td, tn, expand)
    return o.reshape(MS, ATTN_WIDTH)


def _mlp_kernel(x_ref, g_ref, wu_ref, wd_ref, gf_ref, out_ref, xn_ref, *, final):
    f = pl.program_id(1)

    @pl.when(f == 0)
    def _():
        xn_ref[...] = _rms(x_ref[...], g_ref[...]).astype(BF16)
        out_ref[...] = jnp.zeros_like(out_ref)

    h = jnp.dot(xn_ref[...], wu_ref[...], preferred_element_type=F32)
    h = jnp.maximum(h, 0.0)
    h = (h * h).astype(BF16)
    out_ref[...] += jnp.dot(h, wd_ref[...], preferred_element_type=F32)

    @pl.when(f == pl.num_programs(1) - 1)
    def _():
        y = x_ref[...] + out_ref[...]
        if final:
            y = _rms(y, gf_ref[...])
        out_ref[...] = y


def _mlp(x, gain, w_up, w_down, layer, final_gain, tm, tf, final):
    m = x.shape[0]
    est = 4 * tm * D_MODEL * 4 + 2 * (2 * D_MODEL * tf * 2) + tm * D_MODEL * 2 + tm * tf * 6
    return pl.pallas_call(
        functools.partial(_mlp_kernel, final=final),
        grid=(m // tm, D_FF // tf),
        in_specs=[
            pl.BlockSpec((tm, D_MODEL), lambda i, f: (i, 0)),
            pl.BlockSpec((1, D_MODEL), lambda i, f: (0, 0)),
            pl.BlockSpec((None, D_MODEL, tf), lambda i, f: (layer, 0, f)),
            pl.BlockSpec((None, tf, D_MODEL), lambda i, f: (layer, f, 0)),
            pl.BlockSpec((1, D_MODEL), lambda i, f: (0, 0)),
        ],
        out_specs=pl.BlockSpec((tm, D_MODEL), lambda i, f: (i, 0)),
        out_shape=jax.ShapeDtypeStruct((m, D_MODEL), F32),
        scratch_shapes=[pltpu.VMEM((tm, D_MODEL), BF16)],
        compiler_params=_params(("parallel", "arbitrary"), est),
        name="mlp_final" if final else "mlp",
    )(x, gain.reshape(1, D_MODEL), w_up, w_down, final_gain.reshape(1, D_MODEL))


def _pw1_kernel(x_ref, g_ref, wa_ref, wb_ref, ba_ref, bb_ref, u_ref, xn_ref):
    @pl.when(pl.program_id(1) == 0)
    def _():
        xn_ref[...] = _rms(x_ref[...], g_ref[...]).astype(BF16)

    xn = xn_ref[...]
    a = jnp.dot(xn, wa_ref[...], preferred_element_type=F32) + ba_ref[...]
    b = jnp.dot(xn, wb_ref[...], preferred_element_type=F32) + bb_ref[...]
    u_ref[...] = a * (1.0 / (1.0 + jnp.exp(-b)))


def _pw1_glu(x, gain, w, bias, tm, tn):
    m = x.shape[0]
    nj = D_MODEL // tn
    b2 = bias.reshape(1, 2 * D_MODEL)
    est = 2 * (tm * D_MODEL * 4 + 2 * D_MODEL * tn * 2 + tm * tn * 4) + tm * D_MODEL * 2 + 2 * tm * tn * 4
    return pl.pallas_call(
        _pw1_kernel,
        grid=(m // tm, nj),
        in_specs=[
            pl.BlockSpec((tm, D_MODEL), lambda i, j: (i, 0)),
            pl.BlockSpec((1, D_MODEL), lambda i, j: (0, 0)),
            pl.BlockSpec((D_MODEL, tn), lambda i, j: (0, j)),
            pl.BlockSpec((D_MODEL, tn), lambda i, j: (0, nj + j)),
            pl.BlockSpec((1, tn), lambda i, j: (0, j)),
            pl.BlockSpec((1, tn), lambda i, j: (0, nj + j)),
        ],
        out_specs=pl.BlockSpec((tm, tn), lambda i, j: (i, j)),
        out_shape=jax.ShapeDtypeStruct((m, D_MODEL), F32),
        scratch_shapes=[pltpu.VMEM((tm, D_MODEL), BF16)],
        compiler_params=_params(("parallel", "arbitrary"), est),
        name="pw1_glu",
    )(x, gain.reshape(1, D_MODEL), w, w, b2, b2)


CONV_TS = 256
CONV_HALO = 32


def _dw_prompt_kernel(prev_ref, u_ref, w_ref, b_ref, z_ref, ext_ref, sh_ref):
    i = pl.program_id(1)

    @pl.when(i == 0)
    def _():
        ext_ref[0:CONV_HALO, :] = jnp.zeros((CONV_HALO, D_MODEL), F32)

    @pl.when(i > 0)
    def _():
        ext_ref[0:CONV_HALO, :] = prev_ref[...]

    ext_ref[CONV_HALO:, :] = u_ref[...]
    off = CONV_HALO - HIST
    span = CONV_TS + CONV_HALO - 8

    def chunk(c, carry):
        cs = pl.ds(pl.multiple_of(c * 128, 128), 128)
        for phase in range(1, 8):
            sh_ref[phase - 1] = ext_ref[pl.ds(phase, span), cs]
        acc = jnp.zeros((CONV_TS, 128), F32) + b_ref[:, cs]
        for j in range(CONV_WIDTH):
            a, phase = divmod(j + off, 8)
            if phase == 0:
                src = ext_ref[pl.ds(8 * a, CONV_TS), cs]
            else:
                src = sh_ref[phase - 1, pl.ds(8 * a, CONV_TS), :]
            acc = acc + src * w_ref[pl.ds(j, 1), cs]
        z_ref[:, cs] = acc
        return carry

    lax.fori_loop(0, D_MODEL // 128, chunk, 0)


def _dw_prompt(u, w_dw, b_dw):
    u3 = u.reshape(BATCH, SEQ, D_MODEL)
    nt = SEQ // CONV_TS
    per = CONV_TS // CONV_HALO
    est = 2 * (CONV_HALO + 2 * CONV_TS) * D_MODEL * 4 + (CONV_TS + CONV_HALO) * D_MODEL * 4 + 2 * 32 * D_MODEL * 4
    z = pl.pallas_call(
        _dw_prompt_kernel,
        grid=(BATCH, nt),
        in_specs=[
            pl.BlockSpec((None, CONV_HALO, D_MODEL), lambda b, i: (b, jnp.maximum(i * per - 1, 0), 0)),
            pl.BlockSpec((None, CONV_TS, D_MODEL), lambda b, i: (b, i, 0)),
            pl.BlockSpec((CONV_WIDTH, D_MODEL), lambda b, i: (0, 0)),
            pl.BlockSpec((1, D_MODEL), lambda b, i: (0, 0)),
        ],
        out_specs=pl.BlockSpec((None, CONV_TS, D_MODEL), lambda b, i: (b, i, 0)),
        out_shape=jax.ShapeDtypeStruct((BATCH, SEQ, D_MODEL), F32),
        scratch_shapes=[pltpu.VMEM((CONV_TS + CONV_HALO, D_MODEL), F32),
                        pltpu.VMEM((7, CONV_TS + CONV_HALO - 8, 128), F32)],
        compiler_params=_params(("parallel", "arbitrary"), est),
        name="dwconv_prompt",
    )(u3, u3, w_dw, b_dw.reshape(1, D_MODEL))
    return z.reshape(MP, D_MODEL)


DW_BB = 8


def _dw_sample_kernel(hist_ref, u_ref, wh_ref, wu_ref, b_ref, z_ref, new_ref):
    for b in range(DW_BB):
        hist = hist_ref[b]
        u = u_ref[b]
        for t in range(DEC_SEQ):
            z = jnp.sum(wh_ref[t] * hist, axis=0, keepdims=True)
            z = z + jnp.sum(wu_ref[t] * u, axis=0, keepdims=True)
            z_ref[b, t:t + 1, :] = z + b_ref[...]
        new_ref[b, 0:HIST - DEC_SEQ, :] = hist[DEC_SEQ:, :]
        new_ref[b, HIST - DEC_SEQ:, :] = u


def _dw_sample(state, u, w_dw, b_dw):
    zeros = jnp.zeros((DEC_SEQ, D_MODEL), F32)
    wh = jnp.stack([jnp.concatenate([zeros[:t], w_dw[:HIST - t]], axis=0) for t in range(DEC_SEQ)])
    wu = jnp.stack([jnp.concatenate([w_dw[HIST - t:], zeros[:DEC_SEQ - 1 - t]], axis=0) for t in range(DEC_SEQ)])
    u3 = u.reshape(DEC_BATCH, DEC_SEQ, D_MODEL)
    est = 4 * (DW_BB * (32 + 32 + 8 + 8) * D_MODEL * 4) + 2 * DEC_SEQ * 40 * D_MODEL * 4
    z, new_state = pl.pallas_call(
        _dw_sample_kernel,
        grid=(DEC_BATCH // DW_BB,),
        in_specs=[
            pl.BlockSpec((None, DW_BB, HIST, D_MODEL), lambda i: (0, i, 0, 0)),
            pl.BlockSpec((DW_BB, DEC_SEQ, D_MODEL), lambda i: (i, 0, 0)),
            pl.BlockSpec((DEC_SEQ, HIST, D_MODEL), lambda i: (0, 0, 0)),
            pl.BlockSpec((DEC_SEQ, DEC_SEQ, D_MODEL), lambda i: (0, 0, 0)),
            pl.BlockSpec((1, D_MODEL), lambda i: (0, 0)),
        ],
        out_specs=[pl.BlockSpec((DW_BB, DEC_SEQ, D_MODEL), lambda i: (i, 0, 0)),
                   pl.BlockSpec((None, DW_BB, HIST, D_MODEL), lambda i: (0, i, 0, 0))],
        out_shape=[jax.ShapeDtypeStruct((DEC_BATCH, DEC_SEQ, D_MODEL), F32),
                   jax.ShapeDtypeStruct((1, DEC_BATCH, HIST, D_MODEL), F32)],
        compiler_params=_params(("parallel",), est),
        name="dwconv_sample",
    )(state, u3, wh, wu, b_dw.reshape(1, D_MODEL))
    return z.reshape(MS, D_MODEL), new_state


def _ln_pw2_kernel(z_ref, x_ref, lg_ref, lb_ref, w_ref, b_ref, out_ref):
    z = z_ref[...]
    mu = jnp.mean(z, axis=-1, keepdims=True)
    zc = z - mu
    var = jnp.mean(zc * zc, axis=-1, keepdims=True)
    y = zc * lax.rsqrt(var + LN_EPS) * lg_ref[...] + lb_ref[...]
    y = y * (1.0 / (1.0 + jnp.exp(-y)))
    out_ref[...] = x_ref[...] + jnp.dot(y.astype(BF16), w_ref[...], preferred_element_type=F32) + b_ref[...]


def _ln_pw2(z, x, ln_g, ln_b, w, bias, tm):
    m = x.shape[0]
    est = 2 * (3 * tm * D_MODEL * 4 + D_MODEL * D_MODEL * 2) + 2 * tm * D_MODEL * 4
    row = lambda i: (i, 0)
    vec = pl.BlockSpec((1, D_MODEL), lambda i: (0, 0))
    return pl.pallas_call(
        _ln_pw2_kernel,
        grid=(m // tm,),
        in_specs=[pl.BlockSpec((tm, D_MODEL), row), pl.BlockSpec((tm, D_MODEL), row), vec, vec,
                  pl.BlockSpec((D_MODEL, D_MODEL), lambda i: (0, 0)), vec],
        out_specs=pl.BlockSpec((tm, D_MODEL), row),
        out_shape=jax.ShapeDtypeStruct((m, D_MODEL), F32),
        compiler_params=_params(("parallel",), est),
        name="ln_swish_pw2",
    )(z, x, ln_g.reshape(1, D_MODEL), ln_b.reshape(1, D_MODEL), w, bias.reshape(1, D_MODEL))


def kernel(x_prompt, x_sample, cache_kv_g0, cache_kv_g1, cache_kv_g2, state_conv, attn_norm, w_qkv, w_o,
           conv_norm, w_pw1, b_pw1, w_dw, b_dw, conv_ln_g, conv_ln_b, w_pw2, b_pw2, mlp_norm, w_up, w_down,
           final_norm):
    xp = x_prompt.reshape(MP, D_MODEL)
    xs = x_sample.reshape(MS, D_MODEL)

    wqkv = _to_bf16(w_qkv)[0]
    wo = w_o[0].astype(BF16)
    q3p, kv01p, kv2p = _qkv(xp, attn_norm[0], wqkv, tm=1024)
    q3s, kv01s, kv2s = _qkv(xs, attn_norm[0], wqkv, tm=MS)

    o_p = _attn_prompt(q3p, kv01p, kv2p)
    xp = _wo(o_p, xp, wo, tm=512)

    o_s = _attn_sample(q3s, kv01s, kv2s, cache_kv_g0, cache_kv_g1, cache_kv_g2)
    xs = _wo(o_s, xs, wo, tm=MS)

    wu, wd = _to_bf16(w_up), _to_bf16(w_down)
    xp = _mlp(xp, mlp_norm[0], wu, wd, 0, final_norm, tm=1024, tf=512, final=False)
    xs = _mlp(xs, mlp_norm[0], wu, wd, 0, final_norm, tm=MS, tf=1024, final=False)

    w1 = w_pw1[0].astype(BF16)
    w2 = w_pw2[0].astype(BF16)
    up = _pw1_glu(xp, conv_norm[0], w1, b_pw1[0], tm=1024, tn=512)
    us = _pw1_glu(xs, conv_norm[0], w1, b_pw1[0], tm=MS, tn=512)
    zp = _dw_prompt(up, w_dw[0], b_dw[0])
    zs, conv_sample = _dw_sample(state_conv, us, w_dw[0], b_dw[0])
    xp = _ln_pw2(zp, xp, conv_ln_g[0], conv_ln_b[0], w2, b_pw2[0], tm=512)
    xs = _ln_pw2(zs, xs, conv_ln_g[0], conv_ln_b[0], w2, b_pw2[0], tm=MS)

    yp = _mlp(xp, mlp_norm[1], wu, wd, 1, final_norm, tm=1024, tf=512, final=True)
    ys = _mlp(xs, mlp_norm[1], wu, wd, 1, final_norm, tm=MS, tf=1024, final=True)

    def tail(kv, idx, keep):
        view = kv.reshape((-1, BATCH, SEQ, KV_W))
        return view[idx:idx + 1, :, SEQ - keep:].reshape(1, BATCH, keep, 2, N_HEADS, HEAD_DIM)

    def new_rows(kv):
        return kv.reshape(1, DEC_BATCH, DEC_SEQ, 2, N_HEADS, HEAD_DIM)

    conv_prompt = up.reshape(BATCH, SEQ, D_MODEL)[:, SEQ - HIST:][None]
    return (yp.reshape(BATCH, SEQ, D_MODEL), ys.reshape(DEC_BATCH, DEC_SEQ, D_MODEL),
            tail(kv01p, 0, WINDOWS[0]), tail(kv01p, 1, WINDOWS[1]), tail(kv2p, 0, WINDOWS[2]),
            conv_prompt,
            new_rows(kv01s[0]), new_rows(kv01s[1]), new_rows(kv2s),
            conv_sample)
```
